```python
import math
import jax
import jax.numpy as jnp
from jax import lax
import numpy as np

D_MODEL = 1024
BATCH = 16
SEQ = 2048
DEPTH = 2

HEAD_DIM = 64
D_MIX = D_MODEL
POOL_WINDOWS = (2, 4, 8, 16)
POOL_GROUPS = len(POOL_WINDOWS)
POOL_WIDTH = D_MIX // 4
POOL_GROUP_DIM = POOL_WIDTH // POOL_GROUPS
DIFF_WIDTH = D_MIX // 2
DIFF_V_DIM = 2 * HEAD_DIM
DIFF_HEADS = DIFF_WIDTH // DIFF_V_DIM
MOBA_WIDTH = D_MIX - POOL_WIDTH - DIFF_WIDTH
MOBA_HEADS = MOBA_WIDTH // HEAD_DIM
MOBA_BLOCK = 256
MOBA_TOPK = 3
MOBA_Q_CHUNK = 16
ATTN_Q_BLOCK = 128
IN_COLS = POOL_WIDTH + 3 * DIFF_WIDTH + 3 * MOBA_WIDTH
SPLITS = (POOL_WIDTH,
          POOL_WIDTH + DIFF_WIDTH,
          POOL_WIDTH + 2 * DIFF_WIDTH,
          POOL_WIDTH + 3 * DIFF_WIDTH,
          POOL_WIDTH + 3 * DIFF_WIDTH + MOBA_WIDTH,
          POOL_WIDTH + 3 * DIFF_WIDTH + 2 * MOBA_WIDTH)
D_FF = 2816
ROPE_THETA = 10000.0
NORM_EPS = 1e-6
NEG_INF = -1e30

kernel_name = "hybrid_pool_diffattn_moba_macaron"


def rmsnorm(x, g):
    xf = x.astype(jnp.float32)
    y = xf * lax.rsqrt(jnp.mean(xf * xf, axis=-1, keepdims=True) + NORM_EPS)
    return (y * g.astype(jnp.float32)).astype(x.dtype)


def swiglu_ffn(h, w_in, w_out):
    a, b = jnp.split(h @ w_in, 2, axis=-1)
    return (jax.nn.silu(a) * b) @ w_out


def rope_tables(seq):
    inv = ROPE_THETA ** (-jnp.arange(0, HEAD_DIM, 2, dtype=jnp.float32) / HEAD_DIM)
    ang = jnp.arange(seq, dtype=jnp.float32)[:, None] * inv[None, :]
    return jnp.cos(ang), jnp.sin(ang)


def rope(x, cos, sin):
    half = HEAD_DIM // 2
    xf = x.astype(jnp.float32)
    x1, x2 = xf[..., :half], xf[..., half:]
    out = jnp.concatenate([x1 * cos - x2 * sin, x2 * cos + x1 * sin], axis=-1)
    return out.astype(x.dtype)


def pool_mixer(u, pool_w, pool_scale):
    B, S, _ = u.shape
    uf = u.astype(jnp.float32)
    csum = jnp.pad(jnp.cumsum(uf, axis=1), ((0, 0), (1, 0), (0, 0)))
    t = jnp.arange(S)
    outs = []
    for g, w in enumerate(POOL_WINDOWS):
        c = csum[:, :, g * POOL_GROUP_DIM:(g + 1) * POOL_GROUP_DIM]
        start = jnp.maximum(t + 1 - w, 0)
        count = jnp.minimum(t + 1, w).astype(jnp.float32)
        outs.append((c[:, 1:] - c[:, start]) / count[None, :, None])
    pooled = (jnp.concatenate(outs, axis=-1) - uf).astype(u.dtype)
    pooled = pooled.reshape(B, S, POOL_GROUPS, POOL_GROUP_DIM)
    mixed = jnp.einsum('bsgc,gcd->bsgd', pooled, pool_w).reshape(B, S, POOL_WIDTH)
    return mixed * pool_scale


def diff_attention(q, k, v, lam, subln_g, lambda_init, cos, sin):
    B, S = q.shape[:2]
    q = rope(q.transpose(0, 2, 3, 1, 4), cos, sin)
    k = rope(k.transpose(0, 2, 3, 1, 4), cos, sin)
    v = v.transpose(0, 2, 1, 3)
    lf = lam.astype(jnp.float32)
    lam_val = jnp.exp(jnp.sum(lf[0] * lf[1])) - jnp.exp(jnp.sum(lf[2] * lf[3])) + lambda_init
    scale = HEAD_DIM ** -0.5
    nblk = S // ATTN_Q_BLOCK
    qb = q.reshape(B, DIFF_HEADS, 2, nblk, ATTN_Q_BLOCK, HEAD_DIM).transpose(3, 0, 1, 2, 4, 5)
    kpos = jnp.arange(S)

    def one_block(args):
        qblk, i = args
        qpos = i * ATTN_Q_BLOCK + jnp.arange(ATTN_Q_BLOCK)
        s = jnp.einsum('bhmqd,bhmkd->bhmqk', qblk, k).astype(jnp.float32) * scale
        s = jnp.where(kpos[None, :] <= qpos[:, None], s, NEG_INF)
        p = jax.nn.softmax(s, axis=-1)
        pdiff = p[:, :, 0] - lam_val * p[:, :, 1]
        return jnp.einsum('bhqk,bhkd->bhqd', pdiff.astype(v.dtype), v)

    o = lax.map(one_block, (qb, jnp.arange(nblk)))
    o = o.transpose(1, 2, 0, 3, 4).reshape(B, DIFF_HEADS, S, DIFF_V_DIM)
    o = rmsnorm(o, subln_g) * (1.0 - lambda_init)
    return o.transpose(0, 2, 1, 3).reshape(B, S, DIFF_WIDTH)


def moba_attention(q, k, v, cos, sin):
    B, S = q.shape[:2]
    H = MOBA_HEADS
    q = rope(q.transpose(0, 2, 1, 3), cos, sin)
    k = rope(k.transpose(0, 2, 1, 3), cos, sin)
    v = v.transpose(0, 2, 1, 3)
    nb = -(-S // MOBA_BLOCK)
    pad = nb * MOBA_BLOCK - S
    kblk = jnp.pad(k, ((0, 0), (0, 0), (0, pad), (0, 0))).reshape(B, H, nb, MOBA_BLOCK, HEAD_DIM)
    vblk = jnp.pad(v, ((0, 0), (0, 0), (0, pad), (0, 0))).reshape(B, H, nb, MOBA_BLOCK, HEAD_DIM)
    kmean = jnp.mean(kblk.astype(jnp.float32), axis=3)
    n_sel = min(MOBA_TOPK, nb - 1)
    scale = HEAD_DIM ** -0.5
    nq = S // MOBA_Q_CHUNK
    qc = q.reshape(B, H, nq, MOBA_Q_CHUNK, HEAD_DIM).transpose(2, 0, 1, 3, 4)
    bi = jnp.arange(B)[:, None, None, None]
    hi = jnp.arange(H)[None, :, None, None]
    key_off = jnp.arange(MOBA_BLOCK)
    blk_ids = jnp.arange(nb)

    def one_chunk(args):
        qblk, i = args
        qpos = i * MOBA_Q_CHUNK + jnp.arange(MOBA_Q_CHUNK)
        own = qpos // MOBA_BLOCK
        own_idx = jnp.broadcast_to(own[None, None, :, None], (B, H, MOBA_Q_CHUNK, 1))
        causal = (own[:, None] * MOBA_BLOCK + key_off[None, :]) <= qpos[:, None]
        own_valid = jnp.broadcast_to(causal[None, None, :, None, :], (B, H, MOBA_Q_CHUNK, 1, MOBA_BLOCK))
        if n_sel > 0:
            gate = jnp.einsum('bhqd,bhnd->bhqn', qblk.astype(jnp.float32), kmean)
            gate = jnp.where(blk_ids[None, :] < own[:, None], gate, NEG_INF)
            _, sel = lax.top_k(gate, n_sel)
            sel_valid = sel < own[None, None, :, None]
            idx = jnp.concatenate([sel.astype(own_idx.dtype), own_idx], axis=-1)
            valid = jnp.concatenate(
                [jnp.broadcast_to(sel_valid[..., None], (B, H, MOBA_Q_CHUNK, n_sel, MOBA_BLOCK)), own_valid],
                axis=3)
        else:
            idx = own_idx
            valid = own_valid
        kg = kblk[bi, hi, idx]
        vg = vblk[bi, hi, idx]
        s = jnp.einsum('bhqd,bhqnkd->bhqnk', qblk, kg).astype(jnp.float32) * scale
        s = jnp.where(valid, s, NEG_INF)
        p = jax.nn.softmax(s.reshape(B, H, MOBA_Q_CHUNK, -1), axis=-1).reshape(s.shape)
        return jnp.einsum('bhqnk,bhqnkd->bhqd', p.astype(vg.dtype), vg)

    o = lax.map(one_chunk, (qc, jnp.arange(nq)))
    o = o.transpose(1, 2, 0, 3, 4).reshape(B, H, S, HEAD_DIM)
    return o.transpose(0, 2, 1, 3).reshape(B, S, MOBA_WIDTH)


def token_mixing(h, w_in, w_out, pool_w, pool_scale, diff_lambda, diff_subln, lambda_init, cos, sin):
    B, S, _ = h.shape
    proj = h @ w_in
    u, dq, dk, dv, mq, mk, mv = jnp.split(proj, SPLITS, axis=-1)
    ya = pool_mixer(u, pool_w, pool_scale)
    yb = diff_attention(dq.reshape(B, S, DIFF_HEADS, 2, HEAD_DIM),
                        dk.reshape(B, S, DIFF_HEADS, 2, HEAD_DIM),
                        dv.reshape(B, S, DIFF_HEADS, DIFF_V_DIM),
                        diff_lambda, diff_subln, lambda_init, cos, sin)
    yc = moba_attention(mq.reshape(B, S, MOBA_HEADS, HEAD_DIM),
                        mk.reshape(B, S, MOBA_HEADS, HEAD_DIM),
                        mv.reshape(B, S, MOBA_HEADS, HEAD_DIM), cos, sin)
    return jnp.concatenate([ya, yb, yc], axis=-1) @ w_out


def setup_inputs(seed: int = 0) -> dict:
    key = jax.random.key(seed)
    ks = jax.random.split(key, 16)
    f32 = jnp.float32

    def nrm(k, shape, scale):
        return jax.random.normal(k, shape, f32) * scale

    def gain(k, shape):
        return 1.0 + 0.05 * jax.random.normal(k, shape, f32)

    return {
        "x": nrm(ks[0], (BATCH, SEQ, D_MODEL), 1.0),
        "ffn1_norm": gain(ks[1], (DEPTH, D_MODEL)),
        "ffn1_w_in": nrm(ks[2], (DEPTH, D_MODEL, 2 * D_FF), D_MODEL ** -0.5),
        "ffn1_w_out": nrm(ks[3], (DEPTH, D_FF, D_MODEL), D_FF ** -0.5),
        "mix_norm": gain(ks[4], (DEPTH, D_MODEL)),
        "mix_w_in": nrm(ks[5], (DEPTH, D_MODEL, IN_COLS), D_MODEL ** -0.5),
        "mix_w_out": nrm(ks[6], (DEPTH, D_MIX, D_MODEL), D_MIX ** -0.5),
        "pool_w": nrm(ks[7], (DEPTH, POOL_GROUPS, POOL_GROUP_DIM, POOL_GROUP_DIM), POOL_GROUP_DIM ** -0.5),
        "pool_scale": gain(ks[8], (DEPTH, POOL_WIDTH)),
        "diff_lambda": nrm(ks[9], (DEPTH, 4, HEAD_DIM), 0.1),
        "diff_subln": gain(ks[10], (DEPTH, DIFF_V_DIM)),
        "ffn2_norm": gain(ks[11], (DEPTH, D_MODEL)),
        "ffn2_w_in": nrm(ks[12], (DEPTH, D_MODEL, 2 * D_FF), D_MODEL ** -0.5),
        "ffn2_w_out": nrm(ks[13], (DEPTH, D_FF, D_MODEL), D_FF ** -0.5),
        "final_norm": gain(ks[14], (D_MODEL,)),
    }


def reference(x, ffn1_norm, ffn1_w_in, ffn1_w_out, mix_norm, mix_w_in, mix_w_out, pool_w, pool_scale,
              diff_lambda, diff_subln, ffn2_norm, ffn2_w_in, ffn2_w_out, final_norm):
    cos, sin = rope_tables(x.shape[1])
    for l in range(DEPTH):
        lambda_init = 0.8 - 0.6 * math.exp(-0.3 * l)
        x = x + 0.5 * swiglu_ffn(rmsnorm(x, ffn1_norm[l]), ffn1_w_in[l], ffn1_w_out[l])
        x = x + token_mixing(rmsnorm(x, mix_norm[l]), mix_w_in[l], mix_w_out[l], pool_w[l], pool_scale[l],
                             diff_lambda[l], diff_subln[l], lambda_init, cos, sin)
        x = x + 0.5 * swiglu_ffn(rmsnorm(x, ffn2_norm[l]), ffn2_w_in[l], ffn2_w_out[l])
    return rmsnorm(x, final_norm)
```

```python
import functools
import math

import jax
import jax.numpy as jnp
import numpy as np
from jax import lax
from jax.experimental import pallas as pl
from jax.experimental.pallas import tpu as pltpu

HEAD_DIM = 64
POOL_WINDOWS = (2, 4, 8, 16)
POOL_GROUP_DIM = 64
POOL_WIDTH = 256
DIFF_WIDTH = 512
DIFF_HEADS = 4
MOBA_WIDTH = 256
MOBA_BLOCK = 256
MOBA_TOPK = 3
ROPE_THETA = 10000.0
NORM_EPS = 1e-6
NEG_INF = -1e30

LANES = 128
QUARTER = HEAD_DIM // 2
ATT_BLOCK = 256
ROW_TILE = 512
FF_CHUNK = 256
VMEM_LIMIT = 56 * 1024 * 1024

F32 = jnp.float32
BF16 = jnp.bfloat16


def _dot(a, b):
    return jnp.dot(a, b, preferred_element_type=F32)


def _rms(x, g):
    return x * lax.rsqrt(jnp.mean(x * x, axis=-1, keepdims=True) + NORM_EPS) * g


def _resident(shape):
    nd = len(shape)
    return pl.BlockSpec(shape, lambda *_: (0,) * nd, pipeline_mode=pl.Buffered(1))


def _ffn_body(x_ref, g_ref, win_ref, wout_ref, gf_ref, o_ref, u_scr, *, d_ff, final_norm):
    x = x_ref[...]
    h = _rms(x, g_ref[...]).astype(BF16)
    for c in range(d_ff // FF_CHUNK):
        lo = c * FF_CHUNK
        a = _dot(h, win_ref[:, lo:lo + FF_CHUNK])
        b = _dot(h, win_ref[:, d_ff + lo:d_ff + lo + FF_CHUNK])
        u_scr[:, lo:lo + FF_CHUNK] = (a / (1.0 + jnp.exp(-a)) * b).astype(BF16)
    y = x + 0.5 * _dot(u_scr[...], wout_ref[...])
    if final_norm:
        y = _rms(y, gf_ref[...])
    o_ref[...] = y


def _ffn(x, g, w_in, w_out, gf, final_norm):
    t, d = x.shape
    d_ff = w_out.shape[0]
    row = pl.BlockSpec((ROW_TILE, d), lambda i: (i, 0))
    return pl.pallas_call(
        functools.partial(_ffn_body, d_ff=d_ff, final_norm=final_norm),
        grid=(t // ROW_TILE,),
        in_specs=[row, _resident((1, d)), _resident(w_in.shape), _resident(w_out.shape), _resident((1, d))],
        out_specs=row,
        out_shape=jax.ShapeDtypeStruct((t, d), F32),
        scratch_shapes=[pltpu.VMEM((ROW_TILE, d_ff), BF16)],
        compiler_params=pltpu.CompilerParams(dimension_semantics=("parallel",), vmem_limit_bytes=VMEM_LIMIT),
        name="ffn",
    )(x, g, w_in, w_out, gf)


def _inproj_body(x_ref, g_ref, w_ref, cos_ref, sin_ref,
                 u_ref, dq_ref, dk_ref, dv_ref, mq_ref, mk_ref, mv_ref, kmean_ref):
    h = _rms(x_ref[...], g_ref[...]).astype(BF16)
    cos = cos_ref[...]
    sin = sin_ref[...]

    def proj(lo, width=LANES):
        return _dot(h, w_ref[:, lo:lo + width])

    def rope(z):
        return z * cos + pltpu.roll(z, 2 * QUARTER, axis=1) * sin

    u_ref[...] = proj(0, POOL_WIDTH)
    base = POOL_WIDTH
    for ref in (dq_ref, dk_ref):
        for c in range(DIFF_WIDTH // LANES):
            ref[:, c * LANES:(c + 1) * LANES] = rope(proj(base + c * LANES)).astype(BF16)
        base += DIFF_WIDTH
    dv_ref[...] = proj(base, DIFF_WIDTH).astype(BF16)
    base += DIFF_WIDTH
    for c in range(MOBA_WIDTH // LANES):
        mq_ref[:, c * LANES:(c + 1) * LANES] = rope(proj(base + c * LANES)).astype(BF16)
    base += MOBA_WIDTH
    for c in range(MOBA_WIDTH // LANES):
        k = rope(proj(base + c * LANES))
        mk_ref[:, c * LANES:(c + 1) * LANES] = k.astype(BF16)
        for r in range(ROW_TILE // MOBA_BLOCK):
            blk = k[r * MOBA_BLOCK:(r + 1) * MOBA_BLOCK]
            kmean_ref[r, :, c * LANES:(c + 1) * LANES] = jnp.mean(blk, axis=0, keepdims=True)
    base += MOBA_WIDTH
    mv_ref[...] = proj(base, MOBA_WIDTH).astype(BF16)


def _inproj(x, g, w, cos, sin, seq):
    t, d = x.shape
    nt = t // ROW_TILE
    tiles_per_seq = seq // ROW_TILE
    blocks_per_tile = ROW_TILE // MOBA_BLOCK

    def rows(width):
        return pl.BlockSpec((ROW_TILE, width), lambda i: (i, 0))

    table = pl.BlockSpec((ROW_TILE, LANES), lambda i: (i % tiles_per_seq, 0))
    widths = (POOL_WIDTH, DIFF_WIDTH, DIFF_WIDTH, DIFF_WIDTH, MOBA_WIDTH, MOBA_WIDTH, MOBA_WIDTH)
    dtypes = (F32, BF16, BF16, BF16, BF16, BF16, BF16)
    out_shape = [jax.ShapeDtypeStruct((t, wd), dt) for wd, dt in zip(widths, dtypes)]
    out_shape.append(jax.ShapeDtypeStruct((t // MOBA_BLOCK, 1, MOBA_WIDTH), F32))
    out_specs = [rows(wd) for wd in widths]
    out_specs.append(pl.BlockSpec((blocks_per_tile, 1, MOBA_WIDTH), lambda i: (i, 0, 0)))
    return pl.pallas_call(
        _inproj_body,
        grid=(nt,),
        in_specs=[rows(d), _resident((1, d)), _resident(w.shape), table, table],
        out_specs=out_specs,
        out_shape=out_shape,
        compiler_params=pltpu.CompilerParams(dimension_semantics=("parallel",), vmem_limit_bytes=VMEM_LIMIT),
        name="inproj",
    )(x, g, w, cos, sin)


def _pool_body(u_ref, w_ref, scale_ref, o_ref):
    u = u_ref[...]
    s = u.shape[0]
    row = lax.broadcasted_iota(jnp.int32, (s, 1), 0)
    lane_group = lax.broadcasted_iota(jnp.int32, (1, POOL_WIDTH), 1) // POOL_GROUP_DIM
    window_sum = u
    pooled = jnp.zeros_like(u)
    shift = 1
    for gi, wdw in enumerate(POOL_WINDOWS):
        while shift < wdw:
            shifted = jnp.where(row >= shift, pltpu.roll(window_sum, shift, axis=0), 0.0)
            window_sum = window_sum + shifted
            shift *= 2
        count = jnp.minimum(row + 1, wdw).astype(F32)
        pooled = jnp.where(lane_group == gi, window_sum / count, pooled)
    mixed = _dot(pooled - u, w_ref[...])
    o_ref[...] = (mixed * scale_ref[...]).astype(BF16)


def _pool(u, w_blockdiag, scale, seq):
    t = u.shape[0]
    blk = pl.BlockSpec((seq, POOL_WIDTH), lambda b: (b, 0))
    return pl.pallas_call(
        _pool_body,
        grid=(t // seq,),
        in_specs=[blk, _resident(w_blockdiag.shape), _resident((1, POOL_WIDTH))],
        out_specs=blk,
        out_shape=jax.ShapeDtypeStruct((t, POOL_WIDTH), BF16),
        compiler_params=pltpu.CompilerParams(dimension_semantics=("parallel",), vmem_limit_bytes=VMEM_LIMIT),
        name="pool",
    )(u, w_blockdiag, scale)


def _channel_is_first(n):
    ch = lax.broadcasted_iota(jnp.int32, (n, 1), 0)
    return (ch // QUARTER) % 2 == 0


def _transpose_values(v_ref, vt_scr):
    for j in range(vt_scr.shape[0]):
        vt_scr[j] = v_ref[j * ATT_BLOCK:(j + 1) * ATT_BLOCK, :].astype(F32).T.astype(BF16)


def _softmax_start(st, vt):
    m = jnp.max(st, axis=0, keepdims=True)
    p = jnp.exp(st - m)
    return m, jnp.sum(p, axis=0, keepdims=True), _dot(vt, p.astype(BF16))


def _softmax_step(st, vt, m, l, acc):
    m_new = jnp.maximum(m, jnp.max(st, axis=0, keepdims=True))
    alpha = jnp.exp(m - m_new)
    p = jnp.exp(st - m_new)
    return m_new, alpha * l + jnp.sum(p, axis=0, keepdims=True), alpha * acc + _dot(vt, p.astype(BF16))


def _causal_mask():
    key = lax.broadcasted_iota(jnp.int32, (ATT_BLOCK, ATT_BLOCK), 0)
    qry = lax.broadcasted_iota(jnp.int32, (ATT_BLOCK, ATT_BLOCK), 1)
    return key <= qry


def _diff_body(q_ref, k_ref, v_ref, lam_ref, g_ref, o_ref, vt_scr, *, lambda_init):
    i = pl.program_id(2)

    @pl.when(i == 0)
    def _():
        _transpose_values(v_ref, vt_scr)

    qt = q_ref[...].astype(F32).T
    first = _channel_is_first(LANES)
    qts = (jnp.where(first, qt, 0.0).astype(BF16), jnp.where(first, 0.0, qt).astype(BF16))
    causal = _causal_mask()

    kd = k_ref[pl.ds(pl.multiple_of(i * ATT_BLOCK, ATT_BLOCK), ATT_BLOCK), :]
    vd = vt_scr[i]
    state = []
    for qm in qts:
        st = jnp.where(causal, _dot(kd, qm), NEG_INF)
        state.extend(_softmax_start(st, vd))

    def past_block(j, carry):
        kj = k_ref[pl.ds(pl.multiple_of(j * ATT_BLOCK, ATT_BLOCK), ATT_BLOCK), :]
        vj = vt_scr[j]
        out = []
        for mi, qm in enumerate(qts):
            out.extend(_softmax_step(_dot(kj, qm), vj, *carry[3 * mi:3 * mi + 3]))
        return tuple(out)

    m0, l0, acc0, m1, l1, acc1 = lax.fori_loop(0, i, past_block, tuple(state))

    lam = lam_ref[...]
    lam_val = (jnp.exp(jnp.sum(lam[0:1] * lam[1:2], keepdims=True))
               - jnp.exp(jnp.sum(lam[2:3] * lam[3:4], keepdims=True)) + lambda_init)
    ot = acc0 / l0 - lam_val * (acc1 / l1)
    o = _rms(ot.T, g_ref[...]) * (1.0 - lambda_init)
    o_ref[...] = o.astype(BF16)


def _diff_attention(dq, dk, dv, lam, subln, lambda_init, seq):
    t = dq.shape[0]
    nq = seq // ATT_BLOCK
    qblk = pl.BlockSpec((ATT_BLOCK, LANES), lambda b, h, i: (b * nq + i, h))
    kv = pl.BlockSpec((seq, LANES), lambda b, h, i: (b, h))
    return pl.pallas_call(
        functools.partial(_diff_body, lambda_init=lambda_init),
        grid=(t // seq, DIFF_HEADS, nq),
        in_specs=[qblk, kv, kv, _resident(lam.shape), _resident((1, LANES))],
        out_specs=qblk,
        out_shape=jax.ShapeDtypeStruct((t, DIFF_WIDTH), BF16),
        scratch_shapes=[pltpu.VMEM((nq, LANES, ATT_BLOCK), BF16)],
        compiler_params=pltpu.CompilerParams(
            dimension_semantics=("parallel", "parallel", "arbitrary"), vmem_limit_bytes=VMEM_LIMIT),
        name="diff_attn",
    )(dq, dk, dv, lam, subln)


def _moba_allow(gate, own):
    nb = gate.shape[0]
    blk = lax.broadcasted_iota(jnp.int32, (nb, 1), 0)
    rank = jnp.zeros(gate.shape, jnp.int32)
    for other in range(nb):
        g_other = gate[other:other + 1, :]
        beats = (g_other > gate) | ((g_other == gate) & (other < blk))
        rank = rank + jnp.where(beats & (other < own), 1, 0)
    return jnp.where((blk < own) & (rank < MOBA_TOPK), 1.0, 0.0)


def _moba_body(q_ref, k_ref, v_ref, kmean_ref, o_ref, vt_scr, allow_scr):
    i = pl.program_id(2)
    nb = kmean_ref.shape[1]

    @pl.when(i == 0)
    def _():
        _transpose_values(v_ref, vt_scr)

    qt = q_ref[...].astype(F32).T
    first = _channel_is_first(LANES)
    causal = _causal_mask()
    kd = k_ref[pl.ds(pl.multiple_of(i * ATT_BLOCK, ATT_BLOCK), ATT_BLOCK), :]
    vd = vt_scr[i]
    outs = []
    for hd in range(2):
        qh = jnp.where(first, qt, 0.0) if hd == 0 else jnp.where(first, 0.0, qt)
        gate = jnp.dot(kmean_ref[0], qh, preferred_element_type=F32, precision=lax.Precision.HIGHEST)
        allow_scr[hd * nb:(hd + 1) * nb, :] = _moba_allow(gate, i)
        qh = qh.astype(BF16)
        rows = slice(hd * HEAD_DIM, (hd + 1) * HEAD_DIM)
        st = jnp.where(causal, _dot(kd, qh), NEG_INF)
        state = _softmax_start(st, vd[rows])

        def past_block(j, carry, qh=qh, rows=rows, hd=hd):
            kj = k_ref[pl.ds(pl.multiple_of(j * ATT_BLOCK, ATT_BLOCK), ATT_BLOCK), :]
            allowed = allow_scr[pl.ds(hd * nb + j, 1), :] > 0.5
            st = jnp.where(allowed, _dot(kj, qh), NEG_INF)
            return _softmax_step(st, vt_scr[j][rows], *carry)

        _, l, acc = lax.fori_loop(0, i, past_block, state)
        outs.append(acc / l)
    o_ref[...] = jnp.concatenate(outs, axis=0).T.astype(BF16)


def _moba_attention(mq, mk, mv, kmean, seq):
    t = mq.shape[0]
    nq = seq // ATT_BLOCK
    pairs = MOBA_WIDTH // LANES
    qblk = pl.BlockSpec((ATT_BLOCK, LANES), lambda b, p, i: (b * nq + i, p))
    kv = pl.BlockSpec((seq, LANES), lambda b, p, i: (b, p))
    km = pl.BlockSpec((1, nq, LANES), lambda b, p, i: (b, 0, p))
    return pl.pallas_call(
        _moba_body,
        grid=(t // seq, pairs, nq),
        in_specs=[qblk, kv, kv, km],
        out_specs=qblk,
        out_shape=jax.ShapeDtypeStruct((t, MOBA_WIDTH), BF16),
        scratch_shapes=[pltpu.VMEM((nq, LANES, ATT_BLOCK), BF16), pltpu.VMEM((2 * nq, ATT_BLOCK), F32)],
        compiler_params=pltpu.CompilerParams(
            dimension_semantics=("parallel", "parallel", "arbitrary"), vmem_limit_bytes=VMEM_LIMIT),
        name="moba_attn",
    )(mq, mk, mv, kmean)


def _outproj_body(x_ref, ya_ref, yb_ref, yc_ref, w_ref, o_ref):
    lo_b = POOL_WIDTH
    lo_c = POOL_WIDTH + DIFF_WIDTH
    y = (_dot(ya_ref[...], w_ref[0:lo_b, :]) + _dot(yb_ref[...], w_ref[lo_b:lo_c, :])
         + _dot(yc_ref[...], w_ref[lo_c:, :]))
    o_ref[...] = x_ref[...] + y


def _outproj(x, ya, yb, yc, w):
    t, d = x.shape

    def rows(width):
        return pl.BlockSpec((ROW_TILE, width), lambda i: (i, 0))

    return pl.pallas_call(
        _outproj_body,
        grid=(t // ROW_TILE,),
        in_specs=[rows(d), rows(POOL_WIDTH), rows(DIFF_WIDTH), rows(MOBA_WIDTH), _resident(w.shape)],
        out_specs=rows(d),
        out_shape=jax.ShapeDtypeStruct((t, d), F32),
        compiler_params=pltpu.CompilerParams(dimension_semantics=("parallel",), vmem_limit_bytes=VMEM_LIMIT),
        name="outproj",
    )(x, ya, yb, yc, w)


def _pair_block_order():
    p = np.arange(LANES)
    quarter, r = p // QUARTER, p % QUARTER
    return (quarter % 2) * HEAD_DIM + (quarter // 2) * QUARTER + r


def _inproj_columns():
    order = _pair_block_order()
    cols, scale = [np.arange(POOL_WIDTH)], [np.ones(POOL_WIDTH)]
    base = POOL_WIDTH
    q_scale = HEAD_DIM ** -0.5
    for width, rotary, sc in ((DIFF_WIDTH, True, q_scale), (DIFF_WIDTH, True, 1.0), (DIFF_WIDTH, False, 1.0),
                              (MOBA_WIDTH, True, q_scale), (MOBA_WIDTH, True, 1.0), (MOBA_WIDTH, False, 1.0)):
        for blk in range(width // LANES):
            inner = order if rotary else np.arange(LANES)
            cols.append(base + blk * LANES + inner)
            scale.append(np.full(LANES, sc))
        base += width
    return np.concatenate(cols), np.concatenate(scale).astype(np.float32)


def _rope_tables(seq):
    inv = ROPE_THETA ** (-jnp.arange(0, HEAD_DIM, 2, dtype=F32) / HEAD_DIM)
    ang = jnp.arange(seq, dtype=F32)[:, None] * inv[None, :]
    cos, sin = jnp.cos(ang), jnp.sin(ang)
    return jnp.tile(cos, (1, 4)), jnp.concatenate([-sin, -sin, sin, sin], axis=1)


def kernel(x, ffn1_norm, ffn1_w_in, ffn1_w_out, mix_norm, mix_w_in, mix_w_out, pool_w, pool_scale,
           diff_lambda, diff_subln, ffn2_norm, ffn2_w_in, ffn2_w_out, final_norm):
    batch, seq, d = x.shape
    depth = ffn1_norm.shape[0]
    t = batch * seq
    assert seq % ROW_TILE == 0 and ROW_TILE % MOBA_BLOCK == 0 and ffn1_w_out.shape[1] % FF_CHUNK == 0

    cos, sin = _rope_tables(seq)
    cols, col_scale = _inproj_columns()
    gf = final_norm.reshape(1, d)
    xt = x.reshape(t, d)
    for l in range(depth):
        lambda_init = 0.8 - 0.6 * math.exp(-0.3 * l)
        last = l == depth - 1
        w_mix_in = (mix_w_in[l][:, cols] * col_scale).astype(BF16)
        pool_bd = jax.scipy.linalg.block_diag(*[pool_w[l, g] for g in range(len(POOL_WINDOWS))])

        xt = _ffn(xt, ffn1_norm[l].reshape(1, d), ffn1_w_in[l].astype(BF16), ffn1_w_out[l].astype(BF16), gf, False)
        u, dq, dk, dv, mq, mk, mv, kmean = _inproj(xt, mix_norm[l].reshape(1, d), w_mix_in, cos, sin, seq)
        ya = _pool(u, pool_bd, pool_scale[l].reshape(1, POOL_WIDTH), seq)
        yb = _diff_attention(dq, dk, dv, diff_lambda[l], diff_subln[l].reshape(1, LANES), lambda_init, seq)
        yc = _moba_attention(mq, mk, mv, kmean.reshape(batch, seq // MOBA_BLOCK, MOBA_WIDTH), seq)
        xt = _outproj(xt, ya, yb, yc, mix_w_out[l].astype(BF16))
        xt = _ffn(xt, ffn2_norm[l].reshape(1, d), ffn2_w_in[l].astype(BF16), ffn2_w_out[l].astype(BF16), gf, last)
    return xt.reshape(batch, seq, d)
```

```python
import functools
import math

import jax
import jax.numpy as jnp
import numpy as np
from jax import lax
from jax.experimental import pallas as pl
from jax.experimental.pallas import tpu as pltpu

HEAD_DIM = 64
POOL_WINDOWS = (2, 4, 8, 16)
POOL_GROUP_DIM = 64
POOL_WIDTH = 256
DIFF_WIDTH = 512
DIFF_HEADS = 4
MOBA_WIDTH = 256
MOBA_BLOCK = 256
MOBA_TOPK = 3
ROPE_THETA = 10000.0
NORM_EPS = 1e-6
NEG_INF = -1e30

LANES = 128
QUARTER = HEAD_DIM // 2
ATT_BLOCK = 256
ROW_TILE = 512
FF_CHUNK = 256
VMEM_LIMIT = 56 * 1024 * 1024

F32 = jnp.float32
BF16 = jnp.bfloat16


def _dot(a, b):
    return jnp.dot(a, b, preferred_element_type=F32)


def _rms(x, g):
    return x * lax.rsqrt(jnp.mean(x * x, axis=-1, keepdims=True) + NORM_EPS) * g


def _resident(shape):
    nd = len(shape)
    return pl.BlockSpec(shape, lambda *_: (0,) * nd, pipeline_mode=pl.Buffered(1))


def _ffn_body(x_ref, g_ref, win_ref, wout_ref, gf_ref, o_ref, u_scr, *, d_ff, final_norm):
    x = x_ref[...]
    h = _rms(x, g_ref[...]).astype(BF16)
    for c in range(d_ff // FF_CHUNK):
        lo = c * FF_CHUNK
        a = _dot(h, win_ref[:, lo:lo + FF_CHUNK])
        b = _dot(h, win_ref[:, d_ff + lo:d_ff + lo + FF_CHUNK])
        u_scr[:, lo:lo + FF_CHUNK] = (a / (1.0 + jnp.exp(-a)) * b).astype(BF16)
    y = x + 0.5 * _dot(u_scr[...], wout_ref[...])
    if final_norm:
        y = _rms(y, gf_ref[...])
    o_ref[...] = y


def _ffn(x, g, w_in, w_out, gf, final_norm):
    t, d = x.shape
    d_ff = w_out.shape[0]
    row = pl.BlockSpec((ROW_TILE, d), lambda i: (i, 0))
    return pl.pallas_call(
        functools.partial(_ffn_body, d_ff=d_ff, final_norm=final_norm),
        grid=(t // ROW_TILE,),
        in_specs=[row, _resident((1, d)), _resident(w_in.shape), _resident(w_out.shape), _resident((1, d))],
        out_specs=row,
        out_shape=jax.ShapeDtypeStruct((t, d), F32),
        scratch_shapes=[pltpu.VMEM((ROW_TILE, d_ff), BF16)],
        compiler_params=pltpu.CompilerParams(dimension_semantics=("parallel",), vmem_limit_bytes=VMEM_LIMIT),
        name="ffn",
    )(x, g, w_in, w_out, gf)


def _inproj_body(x_ref, g_ref, w_ref, cos_ref, sin_ref,
                 u_ref, dq_ref, dk_ref, dv_ref, mq_ref, mk_ref, mv_ref, kmean_ref):
    h = _rms(x_ref[...], g_ref[...]).astype(BF16)
    cos = cos_ref[...]
    sin = sin_ref[...]

    def proj(lo, width=LANES):
        return _dot(h, w_ref[:, lo:lo + width])

    def rope(z):
        return z * cos + pltpu.roll(z, 2 * QUARTER, axis=1) * sin

    u_ref[...] = proj(0, POOL_WIDTH)
    base = POOL_WIDTH
    for ref in (dq_ref, dk_ref):
        for c in range(DIFF_WIDTH // LANES):
            ref[:, c * LANES:(c + 1) * LANES] = rope(proj(base + c * LANES)).astype(BF16)
        base += DIFF_WIDTH
    dv_ref[...] = proj(base, DIFF_WIDTH).astype(BF16)
    base += DIFF_WIDTH
    for c in range(MOBA_WIDTH // LANES):
        mq_ref[:, c * LANES:(c + 1) * LANES] = rope(proj(base + c * LANES)).astype(BF16)
    base += MOBA_WIDTH
    for c in range(MOBA_WIDTH // LANES):
        k = rope(proj(base + c * LANES))
        mk_ref[:, c * LANES:(c + 1) * LANES] = k.astype(BF16)
        for r in range(ROW_TILE // MOBA_BLOCK):
            blk = k[r * MOBA_BLOCK:(r + 1) * MOBA_BLOCK]
            kmean_ref[r, :, c * LANES:(c + 1) * LANES] = jnp.mean(blk, axis=0, keepdims=True)
    base += MOBA_WIDTH
    mv_ref[...] = proj(base, MOBA_WIDTH).astype(BF16)


def _inproj(x, g, w, cos, sin, seq):
    t, d = x.shape
    nt = t // ROW_TILE
    tiles_per_seq = seq // ROW_TILE
    blocks_per_tile = ROW_TILE // MOBA_BLOCK

    def rows(width):
        return pl.BlockSpec((ROW_TILE, width), lambda i: (i, 0))

    table = pl.BlockSpec((ROW_TILE, LANES), lambda i: (i % tiles_per_seq, 0))
    widths = (POOL_WIDTH, DIFF_WIDTH, DIFF_WIDTH, DIFF_WIDTH, MOBA_WIDTH, MOBA_WIDTH, MOBA_WIDTH)
    dtypes = (F32, BF16, BF16, BF16, BF16, BF16, BF16)
    out_shape = [jax.ShapeDtypeStruct((t, wd), dt) for wd, dt in zip(widths, dtypes)]
    out_shape.append(jax.ShapeDtypeStruct((t // MOBA_BLOCK, 1, MOBA_WIDTH), F32))
    out_specs = [rows(wd) for wd in widths]
    out_specs.append(pl.BlockSpec((blocks_per_tile, 1, MOBA_WIDTH), lambda i: (i, 0, 0)))
    return pl.pallas_call(
        _inproj_body,
        grid=(nt,),
        in_specs=[rows(d), _resident((1, d)), _resident(w.shape), table, table],
        out_specs=out_specs,
        out_shape=out_shape,
        compiler_params=pltpu.CompilerParams(dimension_semantics=("parallel",), vmem_limit_bytes=VMEM_LIMIT),
        name="inproj",
    )(x, g, w, cos, sin)


def _pool_body(u_ref, w_ref, scale_ref, o_ref):
    u = u_ref[...]
    s = u.shape[0]
    row = lax.broadcasted_iota(jnp.int32, (s, 1), 0)
    lane_group = lax.broadcasted_iota(jnp.int32, (1, POOL_WIDTH), 1) // POOL_GROUP_DIM
    window_sum = u
    pooled = jnp.zeros_like(u)
    shift = 1
    for gi, wdw in enumerate(POOL_WINDOWS):
        while shift < wdw:
            shifted = jnp.where(row >= shift, pltpu.roll(window_sum, shift, axis=0), 0.0)
            window_sum = window_sum + shifted
            shift *= 2
        count = jnp.minimum(row + 1, wdw).astype(F32)
        pooled = jnp.where(lane_group == gi, window_sum / count, pooled)
    mixed = _dot(pooled - u, w_ref[...])
    o_ref[...] = (mixed * scale_ref[...]).astype(BF16)


def _pool(u, w_blockdiag, scale, seq):
    t = u.shape[0]
    blk = pl.BlockSpec((seq, POOL_WIDTH), lambda b: (b, 0))
    return pl.pallas_call(
        _pool_body,
        grid=(t // seq,),
        in_specs=[blk, _resident(w_blockdiag.shape), _resident((1, POOL_WIDTH))],
        out_specs=blk,
        out_shape=jax.ShapeDtypeStruct((t, POOL_WIDTH), BF16),
        compiler_params=pltpu.CompilerParams(dimension_semantics=("parallel",), vmem_limit_bytes=VMEM_LIMIT),
        name="pool",
    )(u, w_blockdiag, scale)


def _channel_is_first(n):
    ch = lax.broadcasted_iota(jnp.int32, (n, 1), 0)
    return (ch // QUARTER) % 2 == 0


def _blk(j):
    return slice(j * ATT_BLOCK, (j + 1) * ATT_BLOCK)


def _transpose_values(v_ref, vt_scr):
    for j in range(v_ref.shape[0] // ATT_BLOCK):
        vt_scr[:, _blk(j)] = v_ref[_blk(j), :].astype(F32).T.astype(BF16)


def _causal_mask():
    key = lax.broadcasted_iota(jnp.int32, (ATT_BLOCK, ATT_BLOCK), 0)
    qry = lax.broadcasted_iota(jnp.int32, (ATT_BLOCK, ATT_BLOCK), 1)
    return key <= qry


def _attend(i, qt, k_ref, vt, s_scr, p_scr, causal, allow):
    mx = None
    for j in range(i + 1):
        s = _dot(k_ref[_blk(j), :], qt)
        if j == i:
            s = jnp.where(causal, s, NEG_INF)
        elif allow is not None:
            s = jnp.where(allow[j:j + 1, :] > 0.5, s, NEG_INF)
        s_scr[_blk(j), :] = s
        bm = jnp.max(s, axis=0, keepdims=True)
        mx = bm if mx is None else jnp.maximum(mx, bm)
    l = None
    for j in range(i + 1):
        p = jnp.exp(s_scr[_blk(j), :] - mx)
        bl = jnp.sum(p, axis=0, keepdims=True)
        l = bl if l is None else l + bl
        p_scr[_blk(j), :] = p.astype(BF16)
    n = (i + 1) * ATT_BLOCK
    return _dot(vt[:, 0:n], p_scr[0:n, :]), l


def _diff_body(q_ref, k_ref, v_ref, lam_ref, g_ref, o_ref, vt_scr, s_scr, p_scr, *, lambda_init):
    _transpose_values(v_ref, vt_scr)
    first = _channel_is_first(LANES)
    causal = _causal_mask()
    lam = lam_ref[...]
    lam_val = (jnp.exp(jnp.sum(lam[0:1] * lam[1:2], keepdims=True))
               - jnp.exp(jnp.sum(lam[2:3] * lam[3:4], keepdims=True)) + lambda_init)
    gain = g_ref[...] * (1.0 - lambda_init)
    for i in range(q_ref.shape[0] // ATT_BLOCK):
        qt = q_ref[_blk(i), :].astype(F32).T
        outs = []
        for mi in range(2):
            qm = (jnp.where(first, qt, 0.0) if mi == 0 else jnp.where(first, 0.0, qt)).astype(BF16)
            slot = 2 * (i % 2) + mi
            acc, l = _attend(i, qm, k_ref, vt_scr, s_scr.at[slot], p_scr.at[slot], causal, None)
            outs.append(acc * (1.0 / l))
        ot = outs[0] - lam_val * outs[1]
        o_ref[_blk(i), :] = _rms(ot.T, gain).astype(BF16)


def _diff_attention(dq, dk, dv, lam, subln, lambda_init, seq):
    t = dq.shape[0]
    blk = pl.BlockSpec((seq, LANES), lambda b, h: (b, h))
    return pl.pallas_call(
        functools.partial(_diff_body, lambda_init=lambda_init),
        grid=(t // seq, DIFF_HEADS),
        in_specs=[blk, blk, blk, _resident(lam.shape), _resident((1, LANES))],
        out_specs=blk,
        out_shape=jax.ShapeDtypeStruct((t, DIFF_WIDTH), BF16),
        scratch_shapes=[pltpu.VMEM((LANES, seq), BF16), pltpu.VMEM((4, seq, ATT_BLOCK), F32),
                        pltpu.VMEM((4, seq, ATT_BLOCK), BF16)],
        compiler_params=pltpu.CompilerParams(
            dimension_semantics=("parallel", "parallel"), vmem_limit_bytes=VMEM_LIMIT),
        name="diff_attn",
    )(dq, dk, dv, lam, subln)


def _moba_allow(gate, own, n_sel):
    nb = gate.shape[0]
    blk = lax.broadcasted_iota(jnp.int32, (nb, 1), 0)
    rank = jnp.zeros(gate.shape, jnp.int32)
    for other in range(own):
        g_other = gate[other:other + 1, :]
        beats = (g_other > gate) | ((g_other == gate) & (other < blk))
        rank = rank + jnp.where(beats, 1, 0)
    return jnp.where((blk < own) & (rank < n_sel), 1.0, 0.0)


def _moba_body(q_ref, k_ref, v_ref, kmean_ref, o_ref, vt_scr, s_scr, p_scr):
    nq = q_ref.shape[0] // ATT_BLOCK
    n_sel = min(MOBA_TOPK, nq - 1)
    _transpose_values(v_ref, vt_scr)
    first = _channel_is_first(LANES)
    causal = _causal_mask()
    kmean = kmean_ref[0]
    for i in range(nq):
        qt = q_ref[_blk(i), :].astype(F32).T
        outs = []
        for hd in range(2):
            qh = jnp.where(first, qt, 0.0) if hd == 0 else jnp.where(first, 0.0, qt)
            allow = None
            if i > n_sel:
                gate = jnp.dot(kmean, qh, preferred_element_type=F32, precision=lax.Precision.HIGHEST)
                allow = _moba_allow(gate, i, n_sel)
            slot = 2 * (i % 2) + hd
            vt = vt_scr.at[hd * HEAD_DIM:(hd + 1) * HEAD_DIM, :]
            acc, l = _attend(i, qh.astype(BF16), k_ref, vt, s_scr.at[slot], p_scr.at[slot], causal, allow)
            outs.append(acc * (1.0 / l))
        o_ref[_blk(i), :] = jnp.concatenate(outs, axis=0).T.astype(BF16)


def _moba_attention(mq, mk, mv, kmean, seq):
    t = mq.shape[0]
    nq = seq // ATT_BLOCK
    blk = pl.BlockSpec((seq, LANES), lambda b, p: (b, p))
    km = pl.BlockSpec((1, nq, LANES), lambda b, p: (b, 0, p))
    return pl.pallas_call(
        _moba_body,
        grid=(t // seq, MOBA_WIDTH // LANES),
        in_specs=[blk, blk, blk, km],
        out_specs=blk,
        out_shape=jax.ShapeDtypeStruct((t, MOBA_WIDTH), BF16),
        scratch_shapes=[pltpu.VMEM((LANES, seq), BF16), pltpu.VMEM((4, seq, ATT_BLOCK), F32),
                        pltpu.VMEM((4, seq, ATT_BLOCK), BF16)],
        compiler_params=pltpu.CompilerParams(
            dimension_semantics=("parallel", "parallel"), vmem_limit_bytes=VMEM_LIMIT),
        name="moba_attn",
    )(mq, mk, mv, kmean)


def _outproj_body(x_ref, ya_ref, yb_ref, yc_ref, w_ref, o_ref):
    lo_b = POOL_WIDTH
    lo_c = POOL_WIDTH + DIFF_WIDTH
    y = (_dot(ya_ref[...], w_ref[0:lo_b, :]) + _dot(yb_ref[...], w_ref[lo_b:lo_c, :])
         + _dot(yc_ref[...], w_ref[lo_c:, :]))
    o_ref[...] = x_ref[...] + y


def _outproj(x, ya, yb, yc, w):
    t, d = x.shape

    def rows(width):
        return pl.BlockSpec((ROW_TILE, width), lambda i: (i, 0))

    return pl.pallas_call(
        _outproj_body,
        grid=(t // ROW_TILE,),
        in_specs=[rows(d), rows(POOL_WIDTH), rows(DIFF_WIDTH), rows(MOBA_WIDTH), _resident(w.shape)],
        out_specs=rows(d),
        out_shape=jax.ShapeDtypeStruct((t, d), F32),
        compiler_params=pltpu.CompilerParams(dimension_semantics=("parallel",), vmem_limit_bytes=VMEM_LIMIT),
        name="outproj",
    )(x, ya, yb, yc, w)


def _pair_block_order():
    p = np.arange(LANES)
    quarter, r = p // QUARTER, p % QUARTER
    return (quarter % 2) * HEAD_DIM + (quarter // 2) * QUARTER + r


def _inproj_columns():
    order = _pair_block_order()
    cols, scale = [np.arange(POOL_WIDTH)], [np.ones(POOL_WIDTH)]
    base = POOL_WIDTH
    q_scale = HEAD_DIM ** -0.5
    for width, rotary, sc in ((DIFF_WIDTH, True, q_scale), (DIFF_WIDTH, True, 1.0), (DIFF_WIDTH, False, 1.0),
                              (MOBA_WIDTH, True, q_scale), (MOBA_WIDTH, True, 1.0), (MOBA_WIDTH, False, 1.0)):
        for blk in range(width // LANES):
            inner = order if rotary else np.arange(LANES)
            cols.append(base + blk * LANES + inner)
            scale.append(np.full(LANES, sc))
        base += width
    return np.concatenate(cols), np.concatenate(scale).astype(np.float32)


def _rope_tables(seq):
    inv = ROPE_THETA ** (-jnp.arange(0, HEAD_DIM, 2, dtype=F32) / HEAD_DIM)
    ang = jnp.arange(seq, dtype=F32)[:, None] * inv[None, :]
    cos, sin = jnp.cos(ang), jnp.sin(ang)
    return jnp.tile(cos, (1, 4)), jnp.concatenate([-sin, -sin, sin, sin], axis=1)


def kernel(x, ffn1_norm, ffn1_w_in, ffn1_w_out, mix_norm, mix_w_in, mix_w_out, pool_w, pool_scale,
           diff_lambda, diff_subln, ffn2_norm, ffn2_w_in, ffn2_w_out, final_norm):
    batch, seq, d = x.shape
    depth = ffn1_norm.shape[0]
    t = batch * seq
    assert seq % ROW_TILE == 0 and ROW_TILE % MOBA_BLOCK == 0 and ffn1_w_out.shape[1] % FF_CHUNK == 0

    cos, sin = _rope_tables(seq)
    cols, col_scale = _inproj_columns()
    gf = final_norm.reshape(1, d)
    xt = x.reshape(t, d)
    for l in range(depth):
        lambda_init = 0.8 - 0.6 * math.exp(-0.3 * l)
        last = l == depth - 1
        w_mix_in = (mix_w_in[l][:, cols] * col_scale).astype(BF16)
        pool_bd = jax.scipy.linalg.block_diag(*[pool_w[l, g] for g in range(len(POOL_WINDOWS))])

        xt = _ffn(xt, ffn1_norm[l].reshape(1, d), ffn1_w_in[l].astype(BF16), ffn1_w_out[l].astype(BF16), gf, False)
        u, dq, dk, dv, mq, mk, mv, kmean = _inproj(xt, mix_norm[l].reshape(1, d), w_mix_in, cos, sin, seq)
        ya = _pool(u, pool_bd, pool_scale[l].reshape(1, POOL_WIDTH), seq)
        yb = _diff_attention(dq, dk, dv, diff_lambda[l], diff_subln[l].reshape(1, LANES), lambda_init, seq)
        yc = _moba_attention(mq, mk, mv, kmean.reshape(batch, seq // MOBA_BLOCK, MOBA_WIDTH), seq)
        xt = _outproj(xt, ya, yb, yc, mix_w_out[l].astype(BF16))
        xt = _ffn(xt, ffn2_norm[l].reshape(1, d), ffn2_w_in[l].astype(BF16), ffn2_w_out[l].astype(BF16), gf, last)
    return xt.reshape(batch, seq, d)
```

```python
import functools
import math

import jax
import jax.numpy as jnp
import numpy as np
from jax import lax
from jax.experimental import pallas as pl
from jax.experimental.pallas import tpu as pltpu

HEAD_DIM = 64
POOL_WINDOWS = (2, 4, 8, 16)
POOL_GROUP_DIM = 64
POOL_WIDTH = 256
DIFF_WIDTH = 512
DIFF_HEADS = 4
MOBA_WIDTH = 256
MOBA_BLOCK = 256
MOBA_TOPK = 3
ROPE_THETA = 10000.0
NORM_EPS = 1e-6
NEG_INF = -1e30

LANES = 128
MXU_COLS = 256
assert POOL_WIDTH == MXU_COLS and MOBA_WIDTH == MXU_COLS and DIFF_WIDTH % MXU_COLS == 0
QUARTER = HEAD_DIM // 2
ATT_BLOCK = 256
ROW_TILE = 512
FF_CHUNK = 256
ONES_ROWS = 16
VMEM_LIMIT = 56 * 1024 * 1024

F32 = jnp.float32
BF16 = jnp.bfloat16


def _dot(a, b):
    return jnp.dot(a, b, preferred_element_type=F32)


def _rms(x, g):
    return x * lax.rsqrt(jnp.mean(x * x, axis=-1, keepdims=True) + NORM_EPS) * g


def _resident(shape):
    nd = len(shape)
    return pl.BlockSpec(shape, lambda *_: (0,) * nd, pipeline_mode=pl.Buffered(1))


def _ffn_body(x_ref, g_ref, win_ref, wout_ref, gf_ref, *rest, d_ff, final_norm, with_mix):
    if with_mix:
        ya_ref, yb_ref, yc_ref, wmix_ref, o_ref, u_scr = rest
        lo_b, lo_c = POOL_WIDTH, POOL_WIDTH + DIFF_WIDTH
        x = (x_ref[...] + _dot(ya_ref[...], wmix_ref[0:lo_b, :]) + _dot(yb_ref[...], wmix_ref[lo_b:lo_c, :])
             + _dot(yc_ref[...], wmix_ref[lo_c:, :]))
    else:
        o_ref, u_scr = rest
        x = x_ref[...]
    h = _rms(x, g_ref[...]).astype(BF16)
    for c in range(d_ff // FF_CHUNK):
        lo = c * FF_CHUNK
        a = _dot(h, win_ref[:, lo:lo + FF_CHUNK])
        b = _dot(h, win_ref[:, d_ff + lo:d_ff + lo + FF_CHUNK])
        u_scr[:, lo:lo + FF_CHUNK] = (a / (1.0 + jnp.exp(-a)) * b).astype(BF16)
    y = x + 0.5 * _dot(u_scr[...], wout_ref[...])
    if final_norm:
        y = _rms(y, gf_ref[...])
    o_ref[...] = y


def _ffn(x, g, w_in, w_out, gf, final_norm, mix=None):
    t, d = x.shape
    d_ff = w_out.shape[0]

    def rows(width):
        return pl.BlockSpec((ROW_TILE, width), lambda i: (i, 0))

    args = [x, g, w_in, w_out, gf]
    in_specs = [rows(d), _resident((1, d)), _resident(w_in.shape), _resident(w_out.shape), _resident((1, d))]
    if mix is not None:
        args.extend(mix)
        in_specs.extend([rows(POOL_WIDTH), rows(DIFF_WIDTH), rows(MOBA_WIDTH), _resident(mix[3].shape)])
    return pl.pallas_call(
        functools.partial(_ffn_body, d_ff=d_ff, final_norm=final_norm, with_mix=mix is not None),
        grid=(t // ROW_TILE,),
        in_specs=in_specs,
        out_specs=rows(d),
        out_shape=jax.ShapeDtypeStruct((t, d), F32),
        scratch_shapes=[pltpu.VMEM((ROW_TILE, d_ff), BF16)],
        compiler_params=pltpu.CompilerParams(dimension_semantics=("parallel",), vmem_limit_bytes=VMEM_LIMIT),
        name="mix_ffn" if mix is not None else "ffn",
    )(*args)


def _inproj_body(x_ref, g_ref, w_ref, cos_ref, sin_ref,
                 u_ref, dq_ref, dk_ref, dv_ref, mq_ref, mk_ref, mv_ref, kmean_ref):
    h = _rms(x_ref[...], g_ref[...]).astype(BF16)
    cos = cos_ref[...]
    sin = sin_ref[...]

    def proj(lo):
        return _dot(h, w_ref[:, lo:lo + MXU_COLS])

    def rope(z):
        halves = [z[:, c * LANES:(c + 1) * LANES] for c in range(MXU_COLS // LANES)]
        return jnp.concatenate([zc * cos + pltpu.roll(zc, 2 * QUARTER, axis=1) * sin for zc in halves], axis=1)

    u_ref[...] = proj(0)
    base = POOL_WIDTH
    for ref in (dq_ref, dk_ref):
        for c in range(DIFF_WIDTH // MXU_COLS):
            ref[:, c * MXU_COLS:(c + 1) * MXU_COLS] = rope(proj(base + c * MXU_COLS)).astype(BF16)
        base += DIFF_WIDTH
    for c in range(DIFF_WIDTH // MXU_COLS):
        dv_ref[:, c * MXU_COLS:(c + 1) * MXU_COLS] = proj(base + c * MXU_COLS).astype(BF16)
    base += DIFF_WIDTH
    mq_ref[...] = rope(proj(base)).astype(BF16)
    base += MOBA_WIDTH
    k = rope(proj(base))
    mk_ref[...] = k.astype(BF16)
    for r in range(ROW_TILE // MOBA_BLOCK):
        kmean_ref[r] = jnp.mean(k[r * MOBA_BLOCK:(r + 1) * MOBA_BLOCK], axis=0, keepdims=True)
    base += MOBA_WIDTH
    mv_ref[...] = proj(base).astype(BF16)


def _inproj(x, g, w, cos, sin, seq):
    t, d = x.shape
    nt = t // ROW_TILE
    tiles_per_seq = seq // ROW_TILE
    blocks_per_tile = ROW_TILE // MOBA_BLOCK

    def rows(width):
        return pl.BlockSpec((ROW_TILE, width), lambda i: (i, 0))

    table = pl.BlockSpec((ROW_TILE, LANES), lambda i: (i % tiles_per_seq, 0))
    widths = (POOL_WIDTH, DIFF_WIDTH, DIFF_WIDTH, DIFF_WIDTH, MOBA_WIDTH, MOBA_WIDTH, MOBA_WIDTH)
    dtypes = (F32, BF16, BF16, BF16, BF16, BF16, BF16)
    out_shape = [jax.ShapeDtypeStruct((t, wd), dt) for wd, dt in zip(widths, dtypes)]
    out_shape.append(jax.ShapeDtypeStruct((t // MOBA_BLOCK, 1, MOBA_WIDTH), F32))
    out_specs = [rows(wd) for wd in widths]
    out_specs.append(pl.BlockSpec((blocks_per_tile, 1, MOBA_WIDTH), lambda i: (i, 0, 0)))
    return pl.pallas_call(
        _inproj_body,
        grid=(nt,),
        in_specs=[rows(d), _resident((1, d)), _resident(w.shape), table, table],
        out_specs=out_specs,
        out_shape=out_shape,
        compiler_params=pltpu.CompilerParams(dimension_semantics=("parallel",), vmem_limit_bytes=VMEM_LIMIT),
        name="inproj",
    )(x, g, w, cos, sin)


def _pool_body(u_ref, w_ref, scale_ref, o_ref):
    u = u_ref[...]
    s = u.shape[0]
    row = lax.broadcasted_iota(jnp.int32, (s, 1), 0)
    lane_group = lax.broadcasted_iota(jnp.int32, (1, POOL_WIDTH), 1) // POOL_GROUP_DIM
    window_sum = u
    pooled = jnp.zeros_like(u)
    shift = 1
    for gi, wdw in enumerate(POOL_WINDOWS):
        while shift < wdw:
            shifted = jnp.where(row >= shift, pltpu.roll(window_sum, shift, axis=0), 0.0)
            window_sum = window_sum + shifted
            shift *= 2
        count = jnp.minimum(row + 1, wdw).astype(F32)
        pooled = jnp.where(lane_group == gi, window_sum / count, pooled)
    mixed = _dot(pooled - u, w_ref[...])
    o_ref[...] = (mixed * scale_ref[...]).astype(BF16)


def _pool(u, w_blockdiag, scale, seq):
    t = u.shape[0]
    blk = pl.BlockSpec((seq, POOL_WIDTH), lambda b: (b, 0))
    return pl.pallas_call(
        _pool_body,
        grid=(t // seq,),
        in_specs=[blk, _resident(w_blockdiag.shape), _resident((1, POOL_WIDTH))],
        out_specs=blk,
        out_shape=jax.ShapeDtypeStruct((t, POOL_WIDTH), BF16),
        compiler_params=pltpu.CompilerParams(dimension_semantics=("parallel",), vmem_limit_bytes=VMEM_LIMIT),
        name="pool",
    )(u, w_blockdiag, scale)


def _channel_is_first(n):
    ch = lax.broadcasted_iota(jnp.int32, (n, 1), 0)
    return (ch // QUARTER) % 2 == 0


def _blk(j):
    return slice(j * ATT_BLOCK, (j + 1) * ATT_BLOCK)


def _transpose_values(v_ref, vt_scr, row_groups):
    seq = v_ref.shape[0]
    for j in range(seq // ATT_BLOCK):
        vt = v_ref[_blk(j), :].astype(F32).T.astype(BF16)
        base = 0
        for lo, hi in row_groups:
            vt_scr[base:base + hi - lo, _blk(j)] = vt[lo:hi]
            base += hi - lo + ONES_ROWS
    base = 0
    for lo, hi in row_groups:
        base += hi - lo
        vt_scr[base:base + ONES_ROWS, :] = jnp.ones((ONES_ROWS, seq), BF16)
        base += ONES_ROWS


def _causal_mask():
    key = lax.broadcasted_iota(jnp.int32, (ATT_BLOCK, ATT_BLOCK), 0)
    qry = lax.broadcasted_iota(jnp.int32, (ATT_BLOCK, ATT_BLOCK), 1)
    return key <= qry


class _Stream:
    def __init__(self, i, make_qt, vt, slot, tag):
        self.i, self.make_qt, self.vt, self.slot, self.tag = i, make_qt, vt, slot, tag
        self.mx = self.acc = None


def _score_phase(st, k_ref, s_scr, causal):
    qt, allow = st.make_qt()
    mx = None
    for j in range(st.i + 1):
        s = _dot(k_ref[_blk(j), :], qt)
        if j == st.i:
            s = jnp.where(causal, s, NEG_INF)
        elif allow is not None:
            s = jnp.where(allow[j:j + 1, :] > 0.5, s, NEG_INF)
        s_scr[st.slot, _blk(j), :] = s
        bm = jnp.max(s, axis=0, keepdims=True)
        mx = bm if mx is None else jnp.maximum(mx, bm)
        yield
    st.mx = mx


def _value_phase(st, s_scr, p_scr):
    for j in range(st.i + 1):
        p_scr[st.slot, _blk(j), :] = jnp.exp2(s_scr[st.slot, _blk(j), :] - st.mx).astype(BF16)
        yield
    n = (st.i + 1) * ATT_BLOCK
    st.acc = _dot(st.vt[:, 0:n], p_scr[st.slot, 0:n, :])


def _run_streams(streams, k_ref, s_scr, p_scr, causal, on_done):
    prev = None
    for st in list(streams) + [None]:
        score = _score_phase(st, k_ref, s_scr, causal) if st is not None else iter(())
        value = _value_phase(prev, s_scr, p_scr) if prev is not None else iter(())
        live = [score, value]
        while live:
            for g in list(live):
                if next(g, _DONE) is _DONE:
                    live.remove(g)
        if prev is not None:
            on_done(prev)
        prev = st


_DONE = object()


def _diff_body(q_ref, k_ref, v_ref, lam_ref, g_ref, o_ref, vt_scr, s_scr, p_scr, *, lambda_init):
    _transpose_values(v_ref, vt_scr, [(0, LANES)])
    first = _channel_is_first(LANES)
    causal = _causal_mask()
    lam = lam_ref[...]
    lam_val = (jnp.exp(jnp.sum(lam[0:1] * lam[1:2], keepdims=True))
               - jnp.exp(jnp.sum(lam[2:3] * lam[3:4], keepdims=True)) + lambda_init)
    gain = g_ref[...] * (1.0 - lambda_init)
    qts, outs = {}, {}

    def stream(i, mi):
        def make_qt():
            if mi == 0:
                qts[i] = q_ref[_blk(i), :].astype(F32).T
            qm = jnp.where(first, qts[i], 0.0) if mi == 0 else jnp.where(first, 0.0, qts[i])
            return qm.astype(BF16), None
        return _Stream(i, make_qt, vt_scr, 2 * (i % 2) + mi, mi)

    def on_done(st):
        outs[st.tag] = st.acc[0:LANES] * (1.0 / st.acc[LANES:LANES + 1])
        if st.tag == 1:
            ot = outs[0] - lam_val * outs[1]
            o_ref[_blk(st.i), :] = _rms(ot.T, gain).astype(BF16)

    nq = q_ref.shape[0] // ATT_BLOCK
    _run_streams([stream(i, mi) for i in range(nq) for mi in range(2)], k_ref, s_scr, p_scr, causal, on_done)


def _diff_attention(dq, dk, dv, lam, subln, lambda_init, seq):
    t = dq.shape[0]
    blk = pl.BlockSpec((seq, LANES), lambda b, h: (b, h))
    return pl.pallas_call(
        functools.partial(_diff_body, lambda_init=lambda_init),
        grid=(t // seq, DIFF_HEADS),
        in_specs=[blk, blk, blk, _resident(lam.shape), _resident((1, LANES))],
        out_specs=blk,
        out_shape=jax.ShapeDtypeStruct((t, DIFF_WIDTH), BF16),
        scratch_shapes=[pltpu.VMEM((LANES + ONES_ROWS, seq), BF16), pltpu.VMEM((4, seq, ATT_BLOCK), F32),
                        pltpu.VMEM((4, seq, ATT_BLOCK), BF16)],
        compiler_params=pltpu.CompilerParams(
            dimension_semantics=("parallel", "parallel"), vmem_limit_bytes=VMEM_LIMIT),
        name="diff_attn",
    )(dq, dk, dv, lam, subln)


def _moba_allow(gate, own, n_sel):
    nb = gate.shape[0]
    blk = lax.broadcasted_iota(jnp.int32, (nb, 1), 0)
    rank = jnp.zeros(gate.shape, jnp.int32)
    for other in range(own):
        g_other = gate[other:other + 1, :]
        beats = (g_other > gate) | ((g_other == gate) & (other < blk))
        rank = rank + jnp.where(beats, 1, 0)
    return jnp.where((blk < own) & (rank < n_sel), 1.0, 0.0)


def _moba_body(q_ref, k_ref, v_ref, kmean_ref, o_ref, vt_scr, s_scr, p_scr):
    nq = q_ref.shape[0] // ATT_BLOCK
    n_sel = min(MOBA_TOPK, nq - 1)
    _transpose_values(v_ref, vt_scr, [(0, HEAD_DIM), (HEAD_DIM, 2 * HEAD_DIM)])
    first = _channel_is_first(LANES)
    causal = _causal_mask()
    kmean = kmean_ref[0]
    head_rows = HEAD_DIM + ONES_ROWS
    qts, outs = {}, {}

    def stream(i, hd):
        def make_qt():
            if hd == 0:
                qts[i] = q_ref[_blk(i), :].astype(F32).T
            qh = jnp.where(first, qts[i], 0.0) if hd == 0 else jnp.where(first, 0.0, qts[i])
            allow = None
            if i > n_sel:
                gate = jnp.dot(kmean, qh, preferred_element_type=F32, precision=lax.Precision.HIGHEST)
                allow = _moba_allow(gate, i, n_sel)
            return qh.astype(BF16), allow
        return _Stream(i, make_qt, vt_scr.at[hd * head_rows:(hd + 1) * head_rows, :], 2 * (i % 2) + hd, hd)

    def on_done(st):
        outs[st.tag] = st.acc[0:HEAD_DIM] * (1.0 / st.acc[HEAD_DIM:HEAD_DIM + 1])
        if st.tag == 1:
            o_ref[_blk(st.i), :] = jnp.concatenate([outs[0], outs[1]], axis=0).T.astype(BF16)

    _run_streams([stream(i, hd) for i in range(nq) for hd in range(2)], k_ref, s_scr, p_scr, causal, on_done)


def _moba_attention(mq, mk, mv, kmean, seq):
    t = mq.shape[0]
    nq = seq // ATT_BLOCK
    blk = pl.BlockSpec((seq, LANES), lambda b, p: (b, p))
    km = pl.BlockSpec((1, nq, LANES), lambda b, p: (b, 0, p))
    return pl.pallas_call(
        _moba_body,
        grid=(t // seq, MOBA_WIDTH // LANES),
        in_specs=[blk, blk, blk, km],
        out_specs=blk,
        out_shape=jax.ShapeDtypeStruct((t, MOBA_WIDTH), BF16),
        scratch_shapes=[pltpu.VMEM((2 * (HEAD_DIM + ONES_ROWS), seq), BF16), pltpu.VMEM((4, seq, ATT_BLOCK), F32),
                        pltpu.VMEM((4, seq, ATT_BLOCK), BF16)],
        compiler_params=pltpu.CompilerParams(
            dimension_semantics=("parallel", "parallel"), vmem_limit_bytes=VMEM_LIMIT),
        name="moba_attn",
    )(mq, mk, mv, kmean)


def _pair_block_order():
    p = np.arange(LANES)
    quarter, r = p // QUARTER, p % QUARTER
    return (quarter % 2) * HEAD_DIM + (quarter // 2) * QUARTER + r


def _inproj_columns():
    order = _pair_block_order()
    cols, scale = [np.arange(POOL_WIDTH)], [np.ones(POOL_WIDTH)]
    base = POOL_WIDTH
    q_scale = HEAD_DIM ** -0.5 * math.log2(math.e)
    for width, rotary, sc in ((DIFF_WIDTH, True, q_scale), (DIFF_WIDTH, True, 1.0), (DIFF_WIDTH, False, 1.0),
                              (MOBA_WIDTH, True, q_scale), (MOBA_WIDTH, True, 1.0), (MOBA_WIDTH, False, 1.0)):
        for blk in range(width // LANES):
            inner = order if rotary else np.arange(LANES)
            cols.append(base + blk * LANES + inner)
            scale.append(np.full(LANES, sc))
        base += width
    return np.concatenate(cols), np.concatenate(scale).astype(np.float32)


def _rope_tables(seq):
    inv = ROPE_THETA ** (-jnp.arange(0, HEAD_DIM, 2, dtype=F32) / HEAD_DIM)
    ang = jnp.arange(seq, dtype=F32)[:, None] * inv[None, :]
    cos, sin = jnp.cos(ang), jnp.sin(ang)
    return jnp.tile(cos, (1, 4)), jnp.concatenate([-sin, -sin, sin, sin], axis=1)


def kernel(x, ffn1_norm, ffn1_w_in, ffn1_w_out, mix_norm, mix_w_in, mix_w_out, pool_w, pool_scale,
           diff_lambda, diff_subln, ffn2_norm, ffn2_w_in, ffn2_w_out, final_norm):
    batch, seq, d = x.shape
    depth = ffn1_norm.shape[0]
    t = batch * seq
    assert seq % ROW_TILE == 0 and ROW_TILE % MOBA_BLOCK == 0 and ffn1_w_out.shape[1] % FF_CHUNK == 0

    cos, sin = _rope_tables(seq)
    cols, col_scale = _inproj_columns()
    gf = final_norm.reshape(1, d)
    xt = x.reshape(t, d)
    for l in range(depth):
        lambda_init = 0.8 - 0.6 * math.exp(-0.3 * l)
        last = l == depth - 1
        w_mix_in = (mix_w_in[l][:, cols] * col_scale).astype(BF16)
        pool_bd = jax.scipy.linalg.block_diag(*[pool_w[l, g] for g in range(len(POOL_WINDOWS))])

        xt = _ffn(xt, ffn1_norm[l].reshape(1, d), ffn1_w_in[l].astype(BF16), ffn1_w_out[l].astype(BF16), gf, False)
        u, dq, dk, dv, mq, mk, mv, kmean = _inproj(xt, mix_norm[l].reshape(1, d), w_mix_in, cos, sin, seq)
        ya = _pool(u, pool_bd, pool_scale[l].reshape(1, POOL_WIDTH), seq)
        yb = _diff_attention(dq, dk, dv, diff_lambda[l], diff_subln[l].reshape(1, LANES), lambda_init, seq)
        yc = _moba_attention(mq, mk, mv, kmean.reshape(batch, seq // MOBA_BLOCK, MOBA_WIDTH), seq)
        xt = _ffn(xt, ffn2_norm[l].reshape(1, d), ffn2_w_in[l].astype(BF16), ffn2_w_out[l].astype(BF16), gf, last,
                  mix=(ya, yb, yc, mix_w_out[l].astype(BF16)))
    return xt.reshape(batch, seq, d)
```

```python
import functools
import math

import jax
import jax.numpy as jnp
from jax import lax
from jax.experimental import pallas as pl
from jax.experimental.pallas import tpu as pltpu

HEAD_DIM = 64
POOL_WINDOWS = (2, 4, 8, 16)
POOL_GROUP_DIM = 64
POOL_WIDTH = 256
DIFF_WIDTH = 512
DIFF_HEADS = 4
MOBA_WIDTH = 256
MOBA_BLOCK = 256
MOBA_TOPK = 3
ROPE_THETA = 10000.0
NORM_EPS = 1e-6
NEG_INF = -1e30

LANES = 128
MXU_COLS = 256
assert POOL_WIDTH == MXU_COLS and MOBA_WIDTH == MXU_COLS and DIFF_WIDTH % MXU_COLS == 0
QUARTER = HEAD_DIM // 2
ATT_BLOCK = 256
ROW_TILE = 512
FF_CHUNK = 256
ONES_ROWS = 16
VMEM_LIMIT = 56 * 1024 * 1024

F32 = jnp.float32
BF16 = jnp.bfloat16


def _dot(a, b):
    return jnp.dot(a, b, preferred_element_type=F32)


def _rms(x, g):
    return x * lax.rsqrt(jnp.mean(x * x, axis=-1, keepdims=True) + NORM_EPS) * g


def _resident(shape):
    nd = len(shape)
    return pl.BlockSpec(shape, lambda *_: (0,) * nd, pipeline_mode=pl.Buffered(1))


def _layer(arr, layer):
    tail = (0,) * (arr.ndim - 1)
    return pl.BlockSpec((None,) + arr.shape[1:], lambda *_: (layer,) + tail, pipeline_mode=pl.Buffered(1))


def _ffn_body(x_ref, g_ref, win_ref, wout_ref, gf_ref, *rest, d_ff, final_norm, with_mix):
    if with_mix:
        ya_ref, yb_ref, yc_ref, wmix_ref, o_ref, u_scr = rest
        lo_b, lo_c = POOL_WIDTH, POOL_WIDTH + DIFF_WIDTH
        x = (x_ref[...] + _dot(ya_ref[...], wmix_ref[0:lo_b, :]) + _dot(yb_ref[...], wmix_ref[lo_b:lo_c, :])
             + _dot(yc_ref[...], wmix_ref[lo_c:, :]))
    else:
        o_ref, u_scr = rest
        x = x_ref[...]
    h = _rms(x, g_ref[...]).astype(BF16)
    for c in range(d_ff // FF_CHUNK):
        lo = c * FF_CHUNK
        a = _dot(h, win_ref[:, lo:lo + FF_CHUNK])
        b = _dot(h, win_ref[:, d_ff + lo:d_ff + lo + FF_CHUNK])
        u_scr[:, lo:lo + FF_CHUNK] = (a / (1.0 + jnp.exp(-a)) * b).astype(BF16)
    y = x + 0.5 * _dot(u_scr[...], wout_ref[...])
    if final_norm:
        y = _rms(y, gf_ref[...])
    o_ref[...] = y


def _ffn(x, g, w_in, w_out, gf, layer, final_norm, mix=None):
    t, d = x.shape
    d_ff = w_out.shape[1]

    def rows(width):
        return pl.BlockSpec((ROW_TILE, width), lambda i: (i, 0))

    args = [x, g, w_in, w_out, gf]
    in_specs = [rows(d), _layer(g, layer), _layer(w_in, layer), _layer(w_out, layer), _resident((1, d))]
    if mix is not None:
        args.extend(mix)
        in_specs.extend([rows(POOL_WIDTH), rows(DIFF_WIDTH), rows(MOBA_WIDTH), _layer(mix[3], layer)])
    return pl.pallas_call(
        functools.partial(_ffn_body, d_ff=d_ff, final_norm=final_norm, with_mix=mix is not None),
        grid=(t // ROW_TILE,),
        in_specs=in_specs,
        out_specs=rows(d),
        out_shape=jax.ShapeDtypeStruct((t, d), F32),
        scratch_shapes=[pltpu.VMEM((ROW_TILE, d_ff), BF16)],
        compiler_params=pltpu.CompilerParams(dimension_semantics=("parallel",), vmem_limit_bytes=VMEM_LIMIT),
        name="mix_ffn" if mix is not None else "ffn",
    )(*args)


def _inproj_body(x_ref, g_ref, w_ref, cos_ref, sin_ref,
                 u_ref, dq_ref, dk_ref, dv_ref, mq_ref, mk_ref, mv_ref, kmean_ref):
    h = _rms(x_ref[...], g_ref[...]).astype(BF16)
    cos = cos_ref[...]
    sin = sin_ref[...]

    def proj(lo):
        return _dot(h, w_ref[:, lo:lo + MXU_COLS])

    def rope(z):
        halves = [z[:, c * LANES:(c + 1) * LANES] for c in range(MXU_COLS // LANES)]
        return jnp.concatenate([zc * cos + pltpu.roll(zc, 2 * QUARTER, axis=1) * sin for zc in halves], axis=1)

    u_ref[...] = proj(0)
    base = POOL_WIDTH
    for ref in (dq_ref, dk_ref):
        for c in range(DIFF_WIDTH // MXU_COLS):
            ref[:, c * MXU_COLS:(c + 1) * MXU_COLS] = rope(proj(base + c * MXU_COLS)).astype(BF16)
        base += DIFF_WIDTH
    for c in range(DIFF_WIDTH // MXU_COLS):
        dv_ref[:, c * MXU_COLS:(c + 1) * MXU_COLS] = proj(base + c * MXU_COLS).astype(BF16)
    base += DIFF_WIDTH
    mq_ref[...] = rope(proj(base)).astype(BF16)
    base += MOBA_WIDTH
    k = rope(proj(base))
    mk_ref[...] = k.astype(BF16)
    for r in range(ROW_TILE // MOBA_BLOCK):
        kmean_ref[r] = jnp.mean(k[r * MOBA_BLOCK:(r + 1) * MOBA_BLOCK], axis=0, keepdims=True)
    base += MOBA_WIDTH
    mv_ref[...] = proj(base).astype(BF16)


def _inproj(x, g, w, cos, sin, layer, seq):
    t, d = x.shape
    nt = t // ROW_TILE
    tiles_per_seq = seq // ROW_TILE
    blocks_per_tile = ROW_TILE // MOBA_BLOCK

    def rows(width):
        return pl.BlockSpec((ROW_TILE, width), lambda i: (i, 0))

    table = pl.BlockSpec((ROW_TILE, LANES), lambda i: (i % tiles_per_seq, 0))
    widths = (POOL_WIDTH, DIFF_WIDTH, DIFF_WIDTH, DIFF_WIDTH, MOBA_WIDTH, MOBA_WIDTH, MOBA_WIDTH)
    dtypes = (F32, BF16, BF16, BF16, BF16, BF16, BF16)
    out_shape = [jax.ShapeDtypeStruct((t, wd), dt) for wd, dt in zip(widths, dtypes)]
    out_shape.append(jax.ShapeDtypeStruct((t // MOBA_BLOCK, 1, MOBA_WIDTH), F32))
    out_specs = [rows(wd) for wd in widths]
    out_specs.append(pl.BlockSpec((blocks_per_tile, 1, MOBA_WIDTH), lambda i: (i, 0, 0)))
    return pl.pallas_call(
        _inproj_body,
        grid=(nt,),
        in_specs=[rows(d), _layer(g, layer), _layer(w, layer), table, table],
        out_specs=out_specs,
        out_shape=out_shape,
        compiler_params=pltpu.CompilerParams(dimension_semantics=("parallel",), vmem_limit_bytes=VMEM_LIMIT),
        name="inproj",
    )(x, g, w, cos, sin)


def _pool_body(u_ref, w_ref, scale_ref, o_ref):
    u = u_ref[...]
    s = u.shape[0]
    row = lax.broadcasted_iota(jnp.int32, (s, 1), 0)
    lane_group = lax.broadcasted_iota(jnp.int32, (1, POOL_WIDTH), 1) // POOL_GROUP_DIM
    window_sum = u
    pooled = jnp.zeros_like(u)
    shift = 1
    for gi, wdw in enumerate(POOL_WINDOWS):
        while shift < wdw:
            shifted = jnp.where(row >= shift, pltpu.roll(window_sum, shift, axis=0), 0.0)
            window_sum = window_sum + shifted
            shift *= 2
        count = jnp.minimum(row + 1, wdw).astype(F32)
        pooled = jnp.where(lane_group == gi, window_sum / count, pooled)
    mixed = _dot(pooled - u, w_ref[...])
    o_ref[...] = (mixed * scale_ref[...]).astype(BF16)


def _pool(u, w_blockdiag, scale, layer, seq):
    t = u.shape[0]
    blk = pl.BlockSpec((seq, POOL_WIDTH), lambda b: (b, 0))
    return pl.pallas_call(
        _pool_body,
        grid=(t // seq,),
        in_specs=[blk, _layer(w_blockdiag, layer), _layer(scale, layer)],
        out_specs=blk,
        out_shape=jax.ShapeDtypeStruct((t, POOL_WIDTH), BF16),
        compiler_params=pltpu.CompilerParams(dimension_semantics=("parallel",), vmem_limit_bytes=VMEM_LIMIT),
        name="pool",
    )(u, w_blockdiag, scale)


def _channel_is_first(n):
    ch = lax.broadcasted_iota(jnp.int32, (n, 1), 0)
    return (ch // QUARTER) % 2 == 0


def _blk(j):
    return slice(j * ATT_BLOCK, (j + 1) * ATT_BLOCK)


def _transpose_values(v_ref, vt_scr, row_groups):
    seq = v_ref.shape[0]
    for j in range(seq // ATT_BLOCK):
        vt = v_ref[_blk(j), :].astype(F32).T.astype(BF16)
        base = 0
        for lo, hi in row_groups:
            vt_scr[base:base + hi - lo, _blk(j)] = vt[lo:hi]
            base += hi - lo + ONES_ROWS
    base = 0
    for lo, hi in row_groups:
        base += hi - lo
        vt_scr[base:base + ONES_ROWS, :] = jnp.ones((ONES_ROWS, seq), BF16)
        base += ONES_ROWS


def _causal_mask():
    key = lax.broadcasted_iota(jnp.int32, (ATT_BLOCK, ATT_BLOCK), 0)
    qry = lax.broadcasted_iota(jnp.int32, (ATT_BLOCK, ATT_BLOCK), 1)
    return key <= qry


class _Stream:
    def __init__(self, i, make_qt, vt, slot, tag):
        self.i, self.make_qt, self.vt, self.slot, self.tag = i, make_qt, vt, slot, tag
        self.mx = self.acc = None


def _score_phase(st, k_ref, s_scr, causal):
    qt, allow = st.make_qt()
    mx = None
    for j in range(st.i + 1):
        s = _dot(k_ref[_blk(j), :], qt)
        if j == st.i:
            s = jnp.where(causal, s, NEG_INF)
        elif allow is not None:
            s = jnp.where(allow[j:j + 1, :] > 0.5, s, NEG_INF)
        s_scr[st.slot, _blk(j), :] = s
        bm = jnp.max(s, axis=0, keepdims=True)
        mx = bm if mx is None else jnp.maximum(mx, bm)
        yield
    st.mx = mx


def _value_phase(st, s_scr, p_scr):
    for j in range(st.i + 1):
        p_scr[st.slot, _blk(j), :] = jnp.exp2(s_scr[st.slot, _blk(j), :] - st.mx).astype(BF16)
        yield
    n = (st.i + 1) * ATT_BLOCK
    st.acc = _dot(st.vt[:, 0:n], p_scr[st.slot, 0:n, :])


def _run_streams(streams, k_ref, s_scr, p_scr, causal, on_done):
    prev = None
    for st in list(streams) + [None]:
        score = _score_phase(st, k_ref, s_scr, causal) if st is not None else iter(())
        value = _value_phase(prev, s_scr, p_scr) if prev is not None else iter(())
        live = [score, value]
        while live:
            for g in list(live):
                if next(g, _DONE) is _DONE:
                    live.remove(g)
        if prev is not None:
            on_done(prev)
        prev = st


_DONE = object()


def _diff_body(q_ref, k_ref, v_ref, lam_ref, g_ref, o_ref, vt_scr, s_scr, p_scr, *, lambda_init):
    _transpose_values(v_ref, vt_scr, [(0, LANES)])
    first = _channel_is_first(LANES)
    causal = _causal_mask()
    lam = lam_ref[...]
    lam_val = (jnp.exp(jnp.sum(lam[0:1] * lam[1:2], keepdims=True))
               - jnp.exp(jnp.sum(lam[2:3] * lam[3:4], keepdims=True)) + lambda_init)
    gain = g_ref[...] * (1.0 - lambda_init)
    qts, outs = {}, {}

    def stream(i, mi):
        def make_qt():
            if mi == 0:
                qts[i] = q_ref[_blk(i), :].astype(F32).T
            qm = jnp.where(first, qts[i], 0.0) if mi == 0 else jnp.where(first, 0.0, qts[i])
            return qm.astype(BF16), None
        return _Stream(i, make_qt, vt_scr, 2 * (i % 2) + mi, mi)

    def on_done(st):
        outs[st.tag] = st.acc[0:LANES] * (1.0 / st.acc[LANES:LANES + 1])
        if st.tag == 1:
            ot = outs[0] - lam_val * outs[1]
            o_ref[_blk(st.i), :] = _rms(ot.T, gain).astype(BF16)

    nq = q_ref.shape[0] // ATT_BLOCK
    order = [stream(i, mi) for i in reversed(range(nq)) for mi in range(2)]
    _run_streams(order, k_ref, s_scr, p_scr, causal, on_done)


def _diff_attention(dq, dk, dv, lam, subln, layer, lambda_init, seq):
    t = dq.shape[0]
    blk = pl.BlockSpec((seq, LANES), lambda b, h: (b, h))
    return pl.pallas_call(
        functools.partial(_diff_body, lambda_init=lambda_init),
        grid=(t // seq, DIFF_HEADS),
        in_specs=[blk, blk, blk, _layer(lam, layer), _layer(subln, layer)],
        out_specs=blk,
        out_shape=jax.ShapeDtypeStruct((t, DIFF_WIDTH), BF16),
        scratch_shapes=[pltpu.VMEM((LANES + ONES_ROWS, seq), BF16), pltpu.VMEM((4, seq, ATT_BLOCK), F32),
                        pltpu.VMEM((4, seq, ATT_BLOCK), BF16)],
        compiler_params=pltpu.CompilerParams(
            dimension_semantics=("parallel", "parallel"), vmem_limit_bytes=VMEM_LIMIT),
        name="diff_attn",
    )(dq, dk, dv, lam, subln)


def _moba_allow(gate, own, n_sel):
    nb = gate.shape[0]
    blk = lax.broadcasted_iota(jnp.int32, (nb, 1), 0)
    rank = jnp.zeros(gate.shape, jnp.int32)
    for other in range(own):
        g_other = gate[other:other + 1, :]
        beats = (g_other > gate) | ((g_other == gate) & (other < blk))
        rank = rank + jnp.where(beats, 1, 0)
    return jnp.where((blk < own) & (rank < n_sel), 1.0, 0.0)


def _moba_body(q_ref, k_ref, v_ref, kmean_ref, o_ref, vt_scr, s_scr, p_scr):
    nq = q_ref.shape[0] // ATT_BLOCK
    n_sel = min(MOBA_TOPK, nq - 1)
    _transpose_values(v_ref, vt_scr, [(0, HEAD_DIM), (HEAD_DIM, 2 * HEAD_DIM)])
    first = _channel_is_first(LANES)
    causal = _causal_mask()
    kmean = kmean_ref[0]
    head_rows = HEAD_DIM + ONES_ROWS
    qts, outs = {}, {}

    def stream(i, hd):
        def make_qt():
            if hd == 0:
                qts[i] = q_ref[_blk(i), :].astype(F32).T
            qh = jnp.where(first, qts[i], 0.0) if hd == 0 else jnp.where(first, 0.0, qts[i])
            allow = None
            if i > n_sel:
                gate = jnp.dot(kmean, qh, preferred_element_type=F32, precision=lax.Precision.HIGHEST)
                allow = _moba_allow(gate, i, n_sel)
            return qh.astype(BF16), allow
        return _Stream(i, make_qt, vt_scr.at[hd * head_rows:(hd + 1) * head_rows, :], 2 * (i % 2) + hd, hd)

    def on_done(st):
        outs[st.tag] = st.acc[0:HEAD_DIM] * (1.0 / st.acc[HEAD_DIM:HEAD_DIM + 1])
        if st.tag == 1:
            o_ref[_blk(st.i), :] = jnp.concatenate([outs[0], outs[1]], axis=0).T.astype(BF16)

    order = [stream(i, hd) for i in reversed(range(nq)) for hd in range(2)]
    _run_streams(order, k_ref, s_scr, p_scr, causal, on_done)


def _moba_attention(mq, mk, mv, kmean, seq):
    t = mq.shape[0]
    nq = seq // ATT_BLOCK
    blk = pl.BlockSpec((seq, LANES), lambda b, p: (b, p))
    km = pl.BlockSpec((1, nq, LANES), lambda b, p: (b, 0, p))
    return pl.pallas_call(
        _moba_body,
        grid=(t // seq, MOBA_WIDTH // LANES),
        in_specs=[blk, blk, blk, km],
        out_specs=blk,
        out_shape=jax.ShapeDtypeStruct((t, MOBA_WIDTH), BF16),
        scratch_shapes=[pltpu.VMEM((2 * (HEAD_DIM + ONES_ROWS), seq), BF16), pltpu.VMEM((4, seq, ATT_BLOCK), F32),
                        pltpu.VMEM((4, seq, ATT_BLOCK), BF16)],
        compiler_params=pltpu.CompilerParams(
            dimension_semantics=("parallel", "parallel"), vmem_limit_bytes=VMEM_LIMIT),
        name="moba_attn",
    )(mq, mk, mv, kmean)


def _inproj_weights(w):
    depth, d, _ = w.shape
    q_scale = HEAD_DIM ** -0.5 * math.log2(math.e)

    def pair_interleave(sec):
        width = sec.shape[-1]
        return sec.reshape(depth, d, width // LANES, 2, 2, QUARTER).swapaxes(3, 4).reshape(depth, d, width)

    parts, base = [], 0
    for width, rotary, scale in ((POOL_WIDTH, False, None),
                                 (DIFF_WIDTH, True, q_scale), (DIFF_WIDTH, True, None), (DIFF_WIDTH, False, None),
                                 (MOBA_WIDTH, True, q_scale), (MOBA_WIDTH, True, None), (MOBA_WIDTH, False, None)):
        sec = w[:, :, base:base + width]
        if rotary:
            sec = pair_interleave(sec)
        if scale is not None:
            sec = sec * scale
        parts.append(sec)
        base += width
    return jnp.concatenate(parts, axis=-1).astype(BF16)


def _rope_tables(seq):
    inv = ROPE_THETA ** (-jnp.arange(0, HEAD_DIM, 2, dtype=F32) / HEAD_DIM)
    ang = jnp.arange(seq, dtype=F32)[:, None] * inv[None, :]
    cos, sin = jnp.cos(ang), jnp.sin(ang)
    return jnp.tile(cos, (1, 4)), jnp.concatenate([-sin, -sin, sin, sin], axis=1)


def kernel(x, ffn1_norm, ffn1_w_in, ffn1_w_out, mix_norm, mix_w_in, mix_w_out, pool_w, pool_scale,
           diff_lambda, diff_subln, ffn2_norm, ffn2_w_in, ffn2_w_out, final_norm):
    batch, seq, d = x.shape
    depth = ffn1_norm.shape[0]
    t = batch * seq
    assert seq % ROW_TILE == 0 and ROW_TILE % MOBA_BLOCK == 0 and ffn1_w_out.shape[1] % FF_CHUNK == 0

    cos, sin = _rope_tables(seq)
    gf = final_norm.reshape(1, d)
    g1, gm, g2 = (g.reshape(depth, 1, d) for g in (ffn1_norm, mix_norm, ffn2_norm))
    w1_in, w1_out, w2_in, w2_out, w_mix_out = (
        w.astype(BF16) for w in (ffn1_w_in, ffn1_w_out, ffn2_w_in, ffn2_w_out, mix_w_out))
    w_mix_in = _inproj_weights(mix_w_in)
    groups = len(POOL_WINDOWS)
    pool_bd = jnp.einsum("lgcd,gh->lgchd", pool_w, jnp.eye(groups, dtype=F32)).reshape(depth, POOL_WIDTH, POOL_WIDTH)
    pool_sc = pool_scale.reshape(depth, 1, POOL_WIDTH)
    subln = diff_subln.reshape(depth, 1, LANES)

    xt = x.reshape(t, d)
    for l in range(depth):
        lambda_init = 0.8 - 0.6 * math.exp(-0.3 * l)
        xt = _ffn(xt, g1, w1_in, w1_out, gf, l, False)
        u, dq, dk, dv, mq, mk, mv, kmean = _inproj(xt, gm, w_mix_in, cos, sin, l, seq)
        ya = _pool(u, pool_bd, pool_sc, l, seq)
        yb = _diff_attention(dq, dk, dv, diff_lambda, subln, l, lambda_init, seq)
        yc = _moba_attention(mq, mk, mv, kmean.reshape(batch, seq // MOBA_BLOCK, MOBA_WIDTH), seq)
        xt = _ffn(xt, g2, w2_in, w2_out, gf, l, l == depth - 1, mix=(ya, yb, yc, w_mix_out))
    return xt.reshape(batch, seq, d)
```

```python
import functools
import math

import jax
import jax.numpy as jnp
from jax import lax
from jax.experimental import pallas as pl
from jax.experimental.pallas import tpu as pltpu

HEAD_DIM = 64
POOL_WINDOWS = (2, 4, 8, 16)
POOL_GROUP_DIM = 64
POOL_WIDTH = 256
DIFF_WIDTH = 512
DIFF_HEADS = 4
MOBA_WIDTH = 256
MOBA_BLOCK = 256
MOBA_TOPK = 3
ROPE_THETA = 10000.0
NORM_EPS = 1e-6
NEG_INF = -1e30

LANES = 128
MXU_COLS = 256
assert POOL_WIDTH == MXU_COLS and MOBA_WIDTH == MXU_COLS and DIFF_WIDTH % MXU_COLS == 0
QUARTER = HEAD_DIM // 2
ATT_BLOCK = 256
ROW_TILE = 1024
FF_CHUNK = 256
SCORE_SHIFT = 40.0
SAFE_BOUND = 70.0
VALUE_LIMIT = 2.0 ** 60
VMEM_LIMIT = 56 * 1024 * 1024

F32 = jnp.float32
BF16 = jnp.bfloat16


def _dot(a, b):
    return jnp.dot(a, b, preferred_element_type=F32)


def _rms(x, g):
    return x * lax.rsqrt(jnp.mean(x * x, axis=-1, keepdims=True) + NORM_EPS) * g


def _resident(shape):
    nd = len(shape)
    return pl.BlockSpec(shape, lambda *_: (0,) * nd, pipeline_mode=pl.Buffered(1))


def _layer(arr, layer):
    tail = (0,) * (arr.ndim - 1)
    return pl.BlockSpec((None,) + arr.shape[1:], lambda *_: (layer,) + tail, pipeline_mode=pl.Buffered(1))


def _ffn_body(x_ref, g_ref, win_ref, wout_ref, gf_ref, *rest, d_ff, final_norm, with_mix):
    if with_mix:
        ya_ref, yb_ref, yc_ref, wmix_ref, o_ref, u_scr = rest
        lo_b, lo_c = POOL_WIDTH, POOL_WIDTH + DIFF_WIDTH
        x = (x_ref[...] + _dot(ya_ref[...], wmix_ref[0:lo_b, :]) + _dot(yb_ref[...], wmix_ref[lo_b:lo_c, :])
             + _dot(yc_ref[...], wmix_ref[lo_c:, :]))
    else:
        o_ref, u_scr = rest
        x = x_ref[...]
    h = _rms(x, g_ref[...]).astype(BF16)
    for c in range(d_ff // FF_CHUNK):
        lo = c * FF_CHUNK
        a = _dot(h, win_ref[:, lo:lo + FF_CHUNK])
        b = _dot(h, win_ref[:, d_ff + lo:d_ff + lo + FF_CHUNK])
        u_scr[:, lo:lo + FF_CHUNK] = (a / (1.0 + jnp.exp(-a)) * b).astype(BF16)
    y = x + 0.5 * _dot(u_scr[...], wout_ref[...])
    if final_norm:
        y = _rms(y, gf_ref[...])
    o_ref[...] = y


def _ffn(x, g, w_in, w_out, gf, layer, final_norm, mix=None):
    t, d = x.shape
    d_ff = w_out.shape[1]

    def rows(width):
        return pl.BlockSpec((ROW_TILE, width), lambda i: (i, 0))

    args = [x, g, w_in, w_out, gf]
    in_specs = [rows(d), _layer(g, layer), _layer(w_in, layer), _layer(w_out, layer), _resident((1, d))]
    if mix is not None:
        args.extend(mix)
        in_specs.extend([rows(POOL_WIDTH), rows(DIFF_WIDTH), rows(MOBA_WIDTH), _layer(mix[3], layer)])
    return pl.pallas_call(
        functools.partial(_ffn_body, d_ff=d_ff, final_norm=final_norm, with_mix=mix is not None),
        grid=(t // ROW_TILE,),
        in_specs=in_specs,
        out_specs=rows(d),
        out_shape=jax.ShapeDtypeStruct((t, d), F32),
        scratch_shapes=[pltpu.VMEM((ROW_TILE, d_ff), BF16)],
        compiler_params=pltpu.CompilerParams(dimension_semantics=("parallel",), vmem_limit_bytes=VMEM_LIMIT),
        name="mix_ffn" if mix is not None else "ffn",
    )(*args)


def _inproj_body(x_ref, g_ref, w_ref, cos_ref, sin_ref,
                 u_ref, dq_ref, dk_ref, dv_ref, mq_ref, mk_ref, mv_ref, kmean_ref):
    h = _rms(x_ref[...], g_ref[...]).astype(BF16)
    cos = cos_ref[...]
    sin = sin_ref[...]

    def proj(lo):
        return _dot(h, w_ref[:, lo:lo + MXU_COLS])

    def rope(z):
        halves = [z[:, c * LANES:(c + 1) * LANES] for c in range(MXU_COLS // LANES)]
        return jnp.concatenate([zc * cos + pltpu.roll(zc, 2 * QUARTER, axis=1) * sin for zc in halves], axis=1)

    u_ref[...] = proj(0)
    base = POOL_WIDTH
    for ref in (dq_ref, dk_ref):
        for c in range(DIFF_WIDTH // MXU_COLS):
            ref[:, c * MXU_COLS:(c + 1) * MXU_COLS] = rope(proj(base + c * MXU_COLS)).astype(BF16)
        base += DIFF_WIDTH
    for c in range(DIFF_WIDTH // MXU_COLS):
        dv_ref[:, c * MXU_COLS:(c + 1) * MXU_COLS] = proj(base + c * MXU_COLS).astype(BF16)
    base += DIFF_WIDTH
    mq_ref[...] = rope(proj(base)).astype(BF16)
    base += MOBA_WIDTH
    k = rope(proj(base))
    mk_ref[...] = k.astype(BF16)
    for r in range(ROW_TILE // MOBA_BLOCK):
        kmean_ref[r] = jnp.mean(k[r * MOBA_BLOCK:(r + 1) * MOBA_BLOCK], axis=0, keepdims=True)
    base += MOBA_WIDTH
    mv_ref[...] = proj(base).astype(BF16)


def _inproj(x, g, w, cos, sin, layer, seq):
    t, d = x.shape
    nt = t // ROW_TILE
    tiles_per_seq = seq // ROW_TILE
    blocks_per_tile = ROW_TILE // MOBA_BLOCK

    def rows(width):
        return pl.BlockSpec((ROW_TILE, width), lambda i: (i, 0))

    table = pl.BlockSpec((ROW_TILE, LANES), lambda i: (i % tiles_per_seq, 0))
    widths = (POOL_WIDTH, DIFF_WIDTH, DIFF_WIDTH, DIFF_WIDTH, MOBA_WIDTH, MOBA_WIDTH, MOBA_WIDTH)
    dtypes = (F32, BF16, BF16, BF16, BF16, BF16, BF16)
    out_shape = [jax.ShapeDtypeStruct((t, wd), dt) for wd, dt in zip(widths, dtypes)]
    out_shape.append(jax.ShapeDtypeStruct((t // MOBA_BLOCK, 1, MOBA_WIDTH), F32))
    out_specs = [rows(wd) for wd in widths]
    out_specs.append(pl.BlockSpec((blocks_per_tile, 1, MOBA_WIDTH), lambda i: (i, 0, 0)))
    return pl.pallas_call(
        _inproj_body,
        grid=(nt,),
        in_specs=[rows(d), _layer(g, layer), _layer(w, layer), table, table],
        out_specs=out_specs,
        out_shape=out_shape,
        compiler_params=pltpu.CompilerParams(dimension_semantics=("parallel",), vmem_limit_bytes=VMEM_LIMIT),
        name="inproj",
    )(x, g, w, cos, sin)


def _pool_body(u_ref, w_ref, scale_ref, o_ref):
    u = u_ref[...]
    s = u.shape[0]
    row = lax.broadcasted_iota(jnp.int32, (s, 1), 0)
    lane_group = lax.broadcasted_iota(jnp.int32, (1, POOL_WIDTH), 1) // POOL_GROUP_DIM
    window_sum = u
    pooled = jnp.zeros_like(u)
    shift = 1
    for gi, wdw in enumerate(POOL_WINDOWS):
        while shift < wdw:
            shifted = jnp.where(row >= shift, pltpu.roll(window_sum, shift, axis=0), 0.0)
            window_sum = window_sum + shifted
            shift *= 2
        count = jnp.minimum(row + 1, wdw).astype(F32)
        pooled = jnp.where(lane_group == gi, window_sum / count, pooled)
    mixed = _dot(pooled - u, w_ref[...])
    o_ref[...] = (mixed * scale_ref[...]).astype(BF16)


def _pool(u, w_blockdiag, scale, layer, seq):
    t = u.shape[0]
    blk = pl.BlockSpec((seq, POOL_WIDTH), lambda b: (b, 0))
    return pl.pallas_call(
        _pool_body,
        grid=(t // seq,),
        in_specs=[blk, _layer(w_blockdiag, layer), _layer(scale, layer)],
        out_specs=blk,
        out_shape=jax.ShapeDtypeStruct((t, POOL_WIDTH), BF16),
        compiler_params=pltpu.CompilerParams(dimension_semantics=("parallel",), vmem_limit_bytes=VMEM_LIMIT),
        name="pool",
    )(u, w_blockdiag, scale)


def _channel_is_first(n):
    ch = lax.broadcasted_iota(jnp.int32, (n, 1), 0)
    return (ch // QUARTER) % 2 == 0


def _blk(j):
    return slice(j * ATT_BLOCK, (j + 1) * ATT_BLOCK)


def _transpose_values(v_ref, vt_scr):
    for j in range(v_ref.shape[0] // ATT_BLOCK):
        vt_scr[:, _blk(j)] = v_ref[_blk(j), :].astype(F32).T.astype(BF16)


def _causal_mask():
    key = lax.broadcasted_iota(jnp.int32, (ATT_BLOCK, ATT_BLOCK), 0)
    qry = lax.broadcasted_iota(jnp.int32, (ATT_BLOCK, ATT_BLOCK), 1)
    return key <= qry


def _score_bounds(q_ref, k_ref):
    nq = k_ref.shape[0] // ATT_BLOCK
    rows, run = [], None
    for j in range(nq):
        m = jnp.max(jnp.abs(k_ref[_blk(j), :].astype(F32)), axis=0, keepdims=True)
        run = m if run is None else jnp.maximum(run, m)
        rows.append(run)
    kabs = jnp.concatenate(rows, axis=0)
    lane_first = (lax.broadcasted_iota(jnp.int32, (1, LANES), 1) // QUARTER) % 2 == 0
    ka = jnp.concatenate([jnp.where(lane_first, kabs, 0.0), jnp.where(lane_first, 0.0, kabs)], axis=0)
    b = lax.dot_general(ka.astype(BF16), jnp.abs(q_ref[...]), (((1,), (1,)), ((), ())), preferred_element_type=F32)
    return b * (1.0 + 2.0 ** -10) + 2.0 ** -10


class _Stream:
    def __init__(self, i, qt, allow, bound, vt, slot, tag):
        self.i, self.qt, self.allow, self.bound, self.vt, self.slot, self.tag = i, qt, allow, bound, vt, slot, tag
        self.mx = self.l = self.acc = None


def _masked_scores(st, j, k_ref, causal):
    s = _dot(k_ref[_blk(j), :], st.qt)
    if j == st.i:
        s = jnp.where(causal, s, NEG_INF)
    elif st.allow is not None:
        s = jnp.where(st.allow[j:j + 1, :] > 0.5, s, NEG_INF)
    return s


def _score_phase(st, k_ref, s_scr, causal):
    mx = None
    for j in range(st.i + 1):
        s = _masked_scores(st, j, k_ref, causal)
        s_scr[st.slot, _blk(j), :] = s
        bm = jnp.max(s, axis=0, keepdims=True)
        mx = bm if mx is None else jnp.maximum(mx, bm)
        yield
    st.mx = mx


def _exp_phase(st, s_scr, p_scr):
    for j in range(st.i + 1):
        _emit_probabilities(st, j, jnp.exp2(s_scr[st.slot, _blk(j), :] - st.mx), p_scr)
        yield


def _bounded_phase(st, k_ref, p_scr, causal):
    for j in range(st.i + 1):
        _emit_probabilities(st, j, jnp.exp2(_masked_scores(st, j, k_ref, causal) - st.bound), p_scr)
        yield


def _emit_probabilities(st, j, p, p_scr):
    p_scr[st.slot, _blk(j), :] = p.astype(BF16)
    bl = jnp.sum(p, axis=0, keepdims=True)
    st.l = bl if j == 0 else st.l + bl


def _value_phase(st, p_scr):
    n = (st.i + 1) * ATT_BLOCK
    st.acc = _dot(st.vt[:, 0:n], p_scr[st.slot, 0:n, :])
    yield


def _run_streams(streams, k_ref, s_scr, p_scr, causal, on_done, bounded):
    def stages(st):
        if bounded:
            return [_bounded_phase(st, k_ref, p_scr, causal), _value_phase(st, p_scr)]
        return [_score_phase(st, k_ref, s_scr, causal), _exp_phase(st, s_scr, p_scr), _value_phase(st, p_scr)]

    depth = 2 if bounded else 3
    staged = [stages(st) for st in streams]
    for tick in range(len(streams) + depth - 1):
        live = [staged[tick - d][d] for d in range(depth) if 0 <= tick - d < len(streams)]
        while live:
            for g in list(live):
                if next(g, _DONE) is _DONE:
                    live.remove(g)
        if tick >= depth - 1:
            on_done(streams[tick - depth + 1])


_DONE = object()


def _attention(make_streams, bounds, k_ref, v_ref, s_scr, p_scr, on_done):
    nq = bounds.shape[0] // 2
    worst = jnp.maximum(bounds[nq - 1:nq, :], bounds[2 * nq - 1:2 * nq, :])
    v_abs = jnp.max(jnp.abs(v_ref[...].astype(F32)))
    safe = jnp.logical_and(jnp.max(worst) <= SAFE_BOUND, v_abs <= VALUE_LIMIT)

    @pl.when(safe)
    def _():
        _run_streams(make_streams(), k_ref, s_scr, p_scr, _causal_mask(), on_done, bounded=True)

    @pl.when(jnp.logical_not(safe))
    def _():
        _run_streams(make_streams(), k_ref, s_scr, p_scr, _causal_mask(), on_done, bounded=False)


def _diff_body(q_ref, k_ref, v_ref, lam_ref, g_ref, o_ref, vt_scr, s_scr, p_scr, *, lambda_init):
    lam = lam_ref[...]
    lam_val = (jnp.exp(jnp.sum(lam[0:1] * lam[1:2], keepdims=True))
               - jnp.exp(jnp.sum(lam[2:3] * lam[3:4], keepdims=True)) + lambda_init)
    gain = g_ref[...] * (1.0 - lambda_init)
    nq = q_ref.shape[0] // ATT_BLOCK
    bounds = _score_bounds(q_ref, k_ref)

    def make_streams():
        _transpose_values(v_ref, vt_scr)
        first = _channel_is_first(LANES)
        streams = []
        for i in reversed(range(nq)):
            qt = q_ref[_blk(i), :].astype(F32).T
            for mi in range(2):
                qm = jnp.where(first, qt, 0.0) if mi == 0 else jnp.where(first, 0.0, qt)
                bound = bounds[mi * nq + i:mi * nq + i + 1, _blk(i)] - SCORE_SHIFT
                streams.append(_Stream(i, qm.astype(BF16), None, bound, vt_scr, 2 * (i % 2) + mi, mi))
        return streams

    outs = {}

    def on_done(st):
        outs[st.tag] = st.acc * (1.0 / st.l)
        if st.tag == 1:
            ot = outs[0] - lam_val * outs[1]
            o_ref[_blk(st.i), :] = _rms(ot.T, gain).astype(BF16)

    _attention(make_streams, bounds, k_ref, v_ref, s_scr, p_scr, on_done)


def _diff_attention(dq, dk, dv, lam, subln, layer, lambda_init, seq):
    t = dq.shape[0]
    blk = pl.BlockSpec((seq, LANES), lambda b, h: (b, h))
    return pl.pallas_call(
        functools.partial(_diff_body, lambda_init=lambda_init),
        grid=(t // seq, DIFF_HEADS),
        in_specs=[blk, blk, blk, _layer(lam, layer), _layer(subln, layer)],
        out_specs=blk,
        out_shape=jax.ShapeDtypeStruct((t, DIFF_WIDTH), BF16),
        scratch_shapes=[pltpu.VMEM((LANES, seq), BF16), pltpu.VMEM((4, seq, ATT_BLOCK), F32),
                        pltpu.VMEM((4, seq, ATT_BLOCK), BF16)],
        compiler_params=pltpu.CompilerParams(
            dimension_semantics=("parallel", "parallel"), vmem_limit_bytes=VMEM_LIMIT),
        name="diff_attn",
    )(dq, dk, dv, lam, subln)


def _moba_allow(gate, own, n_sel):
    nb = gate.shape[0]
    blk = lax.broadcasted_iota(jnp.int32, (nb, 1), 0)
    rank = jnp.zeros(gate.shape, jnp.int32)
    for other in range(own):
        g_other = gate[other:other + 1, :]
        beats = (g_other > gate) | ((g_other == gate) & (other < blk))
        rank = rank + jnp.where(beats, 1, 0)
    return jnp.where((blk < own) & (rank < n_sel), 1.0, 0.0)


def _moba_body(q_ref, k_ref, v_ref, kmean_ref, o_ref, vt_scr, s_scr, p_scr):
    nq = q_ref.shape[0] // ATT_BLOCK
    n_sel = min(MOBA_TOPK, nq - 1)
    bounds = _score_bounds(q_ref, k_ref)

    def make_streams():
        _transpose_values(v_ref, vt_scr)
        first = _channel_is_first(LANES)
        kmean = kmean_ref[0]
        streams = []
        for i in reversed(range(nq)):
            qt = q_ref[_blk(i), :].astype(F32).T
            for hd in range(2):
                qh = jnp.where(first, qt, 0.0) if hd == 0 else jnp.where(first, 0.0, qt)
                allow = None
                if i > n_sel:
                    gate = jnp.dot(kmean, qh, preferred_element_type=F32, precision=lax.Precision.HIGHEST)
                    allow = _moba_allow(gate, i, n_sel)
                bound = bounds[hd * nq + i:hd * nq + i + 1, _blk(i)] - SCORE_SHIFT
                vt = vt_scr.at[hd * HEAD_DIM:(hd + 1) * HEAD_DIM, :]
                streams.append(_Stream(i, qh.astype(BF16), allow, bound, vt, 2 * (i % 2) + hd, hd))
        return streams

    outs = {}

    def on_done(st):
        outs[st.tag] = st.acc * (1.0 / st.l)
        if st.tag == 1:
            o_ref[_blk(st.i), :] = jnp.concatenate([outs[0], outs[1]], axis=0).T.astype(BF16)

    _attention(make_streams, bounds, k_ref, v_ref, s_scr, p_scr, on_done)


def _moba_attention(mq, mk, mv, kmean, seq):
    t = mq.shape[0]
    nq = seq // ATT_BLOCK
    blk = pl.BlockSpec((seq, LANES), lambda b, p: (b, p))
    km = pl.BlockSpec((1, nq, LANES), lambda b, p: (b, 0, p))
    return pl.pallas_call(
        _moba_body,
        grid=(t // seq, MOBA_WIDTH // LANES),
        in_specs=[blk, blk, blk, km],
        out_specs=blk,
        out_shape=jax.ShapeDtypeStruct((t, MOBA_WIDTH), BF16),
        scratch_shapes=[pltpu.VMEM((LANES, seq), BF16), pltpu.VMEM((4, seq, ATT_BLOCK), F32),
                        pltpu.VMEM((4, seq, ATT_BLOCK), BF16)],
        compiler_params=pltpu.CompilerParams(
            dimension_semantics=("parallel", "parallel"), vmem_limit_bytes=VMEM_LIMIT),
        name="moba_attn",
    )(mq, mk, mv, kmean)


def _inproj_weights(w):
    depth, d, _ = w.shape
    q_scale = HEAD_DIM ** -0.5 * math.log2(math.e)

    def pair_interleave(sec):
        width = sec.shape[-1]
        return sec.reshape(depth, d, width // LANES, 2, 2, QUARTER).swapaxes(3, 4).reshape(depth, d, width)

    parts, base = [], 0
    for width, rotary, scale in ((POOL_WIDTH, False, None),
                                 (DIFF_WIDTH, True, q_scale), (DIFF_WIDTH, True, None), (DIFF_WIDTH, False, None),
                                 (MOBA_WIDTH, True, q_scale), (MOBA_WIDTH, True, None), (MOBA_WIDTH, False, None)):
        sec = w[:, :, base:base + width]
        if rotary:
            sec = pair_interleave(sec)
        if scale is not None:
            sec = sec * scale
        parts.append(sec)
        base += width
    return jnp.concatenate(parts, axis=-1).astype(BF16)


def _rope_tables(seq):
    inv = ROPE_THETA ** (-jnp.arange(0, HEAD_DIM, 2, dtype=F32) / HEAD_DIM)
    ang = jnp.arange(seq, dtype=F32)[:, None] * inv[None, :]
    cos, sin = jnp.cos(ang), jnp.sin(ang)
    return jnp.tile(cos, (1, 4)), jnp.concatenate([-sin, -sin, sin, sin], axis=1)


def kernel(x, ffn1_norm, ffn1_w_in, ffn1_w_out, mix_norm, mix_w_in, mix_w_out, pool_w, pool_scale,
           diff_lambda, diff_subln, ffn2_norm, ffn2_w_in, ffn2_w_out, final_norm):
    batch, seq, d = x.shape
    depth = ffn1_norm.shape[0]
    t = batch * seq
    assert seq % ROW_TILE == 0 and ROW_TILE % MOBA_BLOCK == 0 and ffn1_w_out.shape[1] % FF_CHUNK == 0

    cos, sin = _rope_tables(seq)
    gf = final_norm.reshape(1, d)
    g1, gm, g2 = (g.reshape(depth, 1, d) for g in (ffn1_norm, mix_norm, ffn2_norm))
    w1_in, w1_out, w2_in, w2_out, w_mix_out = (
        w.astype(BF16) for w in (ffn1_w_in, ffn1_w_out, ffn2_w_in, ffn2_w_out, mix_w_out))
    w_mix_in = _inproj_weights(mix_w_in)
    groups = len(POOL_WINDOWS)
    pool_bd = jnp.einsum("lgcd,gh->lgchd", pool_w, jnp.eye(groups, dtype=F32)).reshape(depth, POOL_WIDTH, POOL_WIDTH)
    pool_sc = pool_scale.reshape(depth, 1, POOL_WIDTH)
    subln = diff_subln.reshape(depth, 1, LANES)

    xt = x.reshape(t, d)
    for l in range(depth):
        lambda_init = 0.8 - 0.6 * math.exp(-0.3 * l)
        xt = _ffn(xt, g1, w1_in, w1_out, gf, l, False)
        u, dq, dk, dv, mq, mk, mv, kmean = _inproj(xt, gm, w_mix_in, cos, sin, l, seq)
        ya = _pool(u, pool_bd, pool_sc, l, seq)
        yb = _diff_attention(dq, dk, dv, diff_lambda, subln, l, lambda_init, seq)
        yc = _moba_attention(mq, mk, mv, kmean.reshape(batch, seq // MOBA_BLOCK, MOBA_WIDTH), seq)
        xt = _ffn(xt, g2, w2_in, w2_out, gf, l, l == depth - 1, mix=(ya, yb, yc, w_mix_out))
    return xt.reshape(batch, seq, d)
```

```python
import functools
import math

import jax
import jax.numpy as jnp
from jax import lax
from jax.experimental import pallas as pl
from jax.experimental.pallas import tpu as pltpu

HEAD_DIM = 64
POOL_WINDOWS = (2, 4, 8, 16)
POOL_GROUP_DIM = 64
POOL_WIDTH = 256
DIFF_WIDTH = 512
DIFF_HEADS = 4
MOBA_WIDTH = 256
MOBA_BLOCK = 256
MOBA_TOPK = 3
ROPE_THETA = 10000.0
NORM_EPS = 1e-6
NEG_INF = -1e30

LANES = 128
MXU_COLS = 256
assert POOL_WIDTH == MXU_COLS and MOBA_WIDTH == MXU_COLS and DIFF_WIDTH % MXU_COLS == 0
QUARTER = HEAD_DIM // 2
ATT_BLOCK = 256
ROW_TILE = 1024
FF_CHUNK = 256
SCORE_SHIFT = 40.0
SAFE_BOUND = 70.0
VALUE_LIMIT = 2.0 ** 60
VMEM_LIMIT = 56 * 1024 * 1024

F32 = jnp.float32
BF16 = jnp.bfloat16


def _dot(a, b):
    return jnp.dot(a, b, preferred_element_type=F32)


def _rms(x, g):
    return x * lax.rsqrt(jnp.mean(x * x, axis=-1, keepdims=True) + NORM_EPS) * g


def _resident(shape):
    nd = len(shape)
    return pl.BlockSpec(shape, lambda *_: (0,) * nd, pipeline_mode=pl.Buffered(1))


def _layer(arr, layer):
    tail = (0,) * (arr.ndim - 1)
    return pl.BlockSpec((None,) + arr.shape[1:], lambda *_: (layer,) + tail, pipeline_mode=pl.Buffered(1))


def _ffn_body(x_ref, g_ref, win_ref, wout_ref, gf_ref, *rest, d_ff, final_norm, with_mix):
    if with_mix:
        ya_ref, yb_ref, yc_ref, wmix_ref, o_ref, u_scr = rest
        lo_b, lo_c = POOL_WIDTH, POOL_WIDTH + DIFF_WIDTH
        x = (x_ref[...] + _dot(ya_ref[...], wmix_ref[0:lo_b, :]) + _dot(yb_ref[...], wmix_ref[lo_b:lo_c, :])
             + _dot(yc_ref[...], wmix_ref[lo_c:, :]))
    else:
        o_ref, u_scr = rest
        x = x_ref[...]
    h = _rms(x, g_ref[...]).astype(BF16)
    for c in range(d_ff // FF_CHUNK):
        lo = c * FF_CHUNK
        a = _dot(h, win_ref[:, lo:lo + FF_CHUNK])
        b = _dot(h, win_ref[:, d_ff + lo:d_ff + lo + FF_CHUNK])
        u_scr[:, lo:lo + FF_CHUNK] = (a / (1.0 + jnp.exp(-a)) * b).astype(BF16)
    y = x + 0.5 * _dot(u_scr[...], wout_ref[...])
    if final_norm:
        y = _rms(y, gf_ref[...])
    o_ref[...] = y


def _ffn(x, g, w_in, w_out, gf, layer, final_norm, mix=None):
    t, d = x.shape
    d_ff = w_out.shape[1]

    def rows(width):
        return pl.BlockSpec((ROW_TILE, width), lambda i: (i, 0))

    args = [x, g, w_in, w_out, gf]
    in_specs = [rows(d), _layer(g, layer), _layer(w_in, layer), _layer(w_out, layer), _resident((1, d))]
    if mix is not None:
        args.extend(mix)
        in_specs.extend([rows(POOL_WIDTH), rows(DIFF_WIDTH), rows(MOBA_WIDTH), _layer(mix[3], layer)])
    return pl.pallas_call(
        functools.partial(_ffn_body, d_ff=d_ff, final_norm=final_norm, with_mix=mix is not None),
        grid=(t // ROW_TILE,),
        in_specs=in_specs,
        out_specs=rows(d),
        out_shape=jax.ShapeDtypeStruct((t, d), F32),
        scratch_shapes=[pltpu.VMEM((ROW_TILE, d_ff), BF16)],
        compiler_params=pltpu.CompilerParams(dimension_semantics=("parallel",), vmem_limit_bytes=VMEM_LIMIT),
        name="mix_ffn" if mix is not None else "ffn",
    )(*args)


def _inproj_body(x_ref, g_ref, w_ref, cos_ref, sin_ref,
                 u_ref, dq_ref, dk_ref, dv_ref, mq_ref, mk_ref, mv_ref, kmean_ref, dstat_ref, mstat_ref):
    h = _rms(x_ref[...], g_ref[...]).astype(BF16)
    cos = cos_ref[...]
    sin = sin_ref[...]

    def proj(lo):
        return _dot(h, w_ref[:, lo:lo + MXU_COLS])

    def rope(z):
        halves = [z[:, c * LANES:(c + 1) * LANES] for c in range(MXU_COLS // LANES)]
        return jnp.concatenate([zc * cos + pltpu.roll(zc, 2 * QUARTER, axis=1) * sin for zc in halves], axis=1)

    def put_absmax(stat_ref, row, z, lo):
        for r in range(ROW_TILE // ATT_BLOCK):
            m = jnp.max(jnp.abs(z[r * ATT_BLOCK:(r + 1) * ATT_BLOCK].astype(F32)), axis=0, keepdims=True)
            stat_ref[r, row:row + 1, lo:lo + MXU_COLS] = m

    u_ref[...] = proj(0)
    base = POOL_WIDTH
    for c in range(DIFF_WIDTH // MXU_COLS):
        dq_ref[:, c * MXU_COLS:(c + 1) * MXU_COLS] = rope(proj(base + c * MXU_COLS)).astype(BF16)
    base += DIFF_WIDTH
    for ref, row, rotary in ((dk_ref, 0, True), (dv_ref, 1, False)):
        for c in range(DIFF_WIDTH // MXU_COLS):
            z = proj(base + c * MXU_COLS)
            z = (rope(z) if rotary else z).astype(BF16)
            ref[:, c * MXU_COLS:(c + 1) * MXU_COLS] = z
            put_absmax(dstat_ref, row, z, c * MXU_COLS)
        base += DIFF_WIDTH
    mq_ref[...] = rope(proj(base)).astype(BF16)
    base += MOBA_WIDTH
    k = rope(proj(base))
    mk_ref[...] = k.astype(BF16)
    put_absmax(mstat_ref, 0, k.astype(BF16), 0)
    for r in range(ROW_TILE // MOBA_BLOCK):
        kmean_ref[r] = jnp.mean(k[r * MOBA_BLOCK:(r + 1) * MOBA_BLOCK], axis=0, keepdims=True)
    base += MOBA_WIDTH
    v = proj(base).astype(BF16)
    mv_ref[...] = v
    put_absmax(mstat_ref, 1, v, 0)


def _inproj(x, g, w, cos, sin, layer, seq):
    t, d = x.shape
    nt = t // ROW_TILE
    tiles_per_seq = seq // ROW_TILE
    blocks_per_tile = ROW_TILE // MOBA_BLOCK

    def rows(width):
        return pl.BlockSpec((ROW_TILE, width), lambda i: (i, 0))

    table = pl.BlockSpec((ROW_TILE, LANES), lambda i: (i % tiles_per_seq, 0))
    widths = (POOL_WIDTH, DIFF_WIDTH, DIFF_WIDTH, DIFF_WIDTH, MOBA_WIDTH, MOBA_WIDTH, MOBA_WIDTH)
    dtypes = (F32, BF16, BF16, BF16, BF16, BF16, BF16)
    out_shape = [jax.ShapeDtypeStruct((t, wd), dt) for wd, dt in zip(widths, dtypes)]
    out_shape.append(jax.ShapeDtypeStruct((t // MOBA_BLOCK, 1, MOBA_WIDTH), F32))
    out_specs = [rows(wd) for wd in widths]
    out_specs.append(pl.BlockSpec((blocks_per_tile, 1, MOBA_WIDTH), lambda i: (i, 0, 0)))
    for wd in (DIFF_WIDTH, MOBA_WIDTH):
        out_shape.append(jax.ShapeDtypeStruct((t // ATT_BLOCK, 2, wd), F32))
        out_specs.append(pl.BlockSpec((blocks_per_tile, 2, wd), lambda i: (i, 0, 0)))
    return pl.pallas_call(
        _inproj_body,
        grid=(nt,),
        in_specs=[rows(d), _layer(g, layer), _layer(w, layer), table, table],
        out_specs=out_specs,
        out_shape=out_shape,
        compiler_params=pltpu.CompilerParams(dimension_semantics=("parallel",), vmem_limit_bytes=VMEM_LIMIT),
        name="inproj",
    )(x, g, w, cos, sin)


def _pool_body(u_ref, w_ref, scale_ref, o_ref):
    s = u_ref.shape[0]
    row = lax.broadcasted_iota(jnp.int32, (s, 1), 0)
    groups_per_tile = LANES // POOL_GROUP_DIM
    lane_group = lax.broadcasted_iota(jnp.int32, (1, LANES), 1) // POOL_GROUP_DIM
    tiles = []
    for t in range(POOL_WIDTH // LANES):
        u = u_ref[:, t * LANES:(t + 1) * LANES]
        window_sum, shift, pooled = u, 1, None
        for gi in range(groups_per_tile):
            wdw = POOL_WINDOWS[t * groups_per_tile + gi]
            while shift < wdw:
                shifted = jnp.where(row >= shift, pltpu.roll(window_sum, shift, axis=0), 0.0)
                window_sum = window_sum + shifted
                shift *= 2
            mean = window_sum * (1.0 / jnp.minimum(row + 1, wdw).astype(F32))
            pooled = mean if pooled is None else jnp.where(lane_group == gi, mean, pooled)
        tiles.append(pooled - u)
    mixed = _dot(jnp.concatenate(tiles, axis=1), w_ref[...])
    o_ref[...] = (mixed * scale_ref[...]).astype(BF16)


def _pool(u, w_blockdiag, scale, layer, seq):
    t = u.shape[0]
    blk = pl.BlockSpec((seq, POOL_WIDTH), lambda b: (b, 0))
    return pl.pallas_call(
        _pool_body,
        grid=(t // seq,),
        in_specs=[blk, _layer(w_blockdiag, layer), _layer(scale, layer)],
        out_specs=blk,
        out_shape=jax.ShapeDtypeStruct((t, POOL_WIDTH), BF16),
        compiler_params=pltpu.CompilerParams(dimension_semantics=("parallel",), vmem_limit_bytes=VMEM_LIMIT),
        name="pool",
    )(u, w_blockdiag, scale)


def _channel_is_first(n):
    ch = lax.broadcasted_iota(jnp.int32, (n, 1), 0)
    return (ch // QUARTER) % 2 == 0


def _blk(j):
    return slice(j * ATT_BLOCK, (j + 1) * ATT_BLOCK)


def _transpose_values(v_ref, vt_scr):
    for j in range(v_ref.shape[0] // ATT_BLOCK):
        vt_scr[:, _blk(j)] = v_ref[_blk(j), :].astype(F32).T.astype(BF16)


def _causal_mask():
    key = lax.broadcasted_iota(jnp.int32, (ATT_BLOCK, ATT_BLOCK), 0)
    qry = lax.broadcasted_iota(jnp.int32, (ATT_BLOCK, ATT_BLOCK), 1)
    return key <= qry


def _score_bounds(q_ref, stat_ref):
    nq = stat_ref.shape[1] // 2
    rows, run = [], None
    for j in range(nq):
        m = stat_ref[0, 2 * j:2 * j + 1, :]
        run = m if run is None else jnp.maximum(run, m)
        rows.append(run)
    kabs = jnp.concatenate(rows, axis=0)
    lane_first = (lax.broadcasted_iota(jnp.int32, (1, LANES), 1) // QUARTER) % 2 == 0
    ka = jnp.concatenate([jnp.where(lane_first, kabs, 0.0), jnp.where(lane_first, 0.0, kabs)], axis=0)
    b = lax.dot_general(ka.astype(BF16), jnp.abs(q_ref[...]), (((1,), (1,)), ((), ())), preferred_element_type=F32)
    return b * (1.0 + 2.0 ** -10) + 2.0 ** -10


class _Stream:
    def __init__(self, i, qt, allow, bound, vt, slot, tag):
        self.i, self.qt, self.allow, self.bound, self.vt, self.slot, self.tag = i, qt, allow, bound, vt, slot, tag
        self.mx = self.l = self.acc = None
        self.prestored = False


def _masked_scores(st, j, k_ref, s_scr, causal):
    if st.prestored:
        return s_scr[st.slot, _blk(j), :]
    s = _dot(k_ref[_blk(j), :], st.qt)
    if j == st.i:
        s = jnp.where(causal, s, NEG_INF)
    elif st.allow is not None:
        s = jnp.where(st.allow[j:j + 1, :] > 0.5, s, NEG_INF)
    return s


def _prestore_scores(st, k_ref, s_scr):
    causal = _causal_mask()
    for j in range(st.i + 1):
        s_scr[st.slot, _blk(j), :] = _masked_scores(st, j, k_ref, s_scr, causal)
    st.prestored = True


def _score_phase(st, k_ref, s_scr, causal):
    mx = None
    for j in range(st.i + 1):
        s = _masked_scores(st, j, k_ref, s_scr, causal)
        if not st.prestored:
            s_scr[st.slot, _blk(j), :] = s
        bm = jnp.max(s, axis=0, keepdims=True)
        mx = bm if mx is None else jnp.maximum(mx, bm)
        yield
    st.mx = mx


def _exp_phase(st, s_scr, p_scr):
    for j in range(st.i + 1):
        _emit_probabilities(st, j, jnp.exp2(s_scr[st.slot, _blk(j), :] - st.mx), p_scr)
        yield


def _bounded_phase(st, k_ref, s_scr, p_scr, causal):
    for j in range(st.i + 1):
        _emit_probabilities(st, j, jnp.exp2(_masked_scores(st, j, k_ref, s_scr, causal) - st.bound), p_scr)
        yield


def _emit_probabilities(st, j, p, p_scr):
    p_scr[st.slot, _blk(j), :] = p.astype(BF16)
    bl = jnp.sum(p, axis=0, keepdims=True)
    st.l = bl if j == 0 else st.l + bl


def _value_phase(st, p_scr):
    n = (st.i + 1) * ATT_BLOCK
    st.acc = _dot(st.vt[:, 0:n], p_scr[st.slot, 0:n, :])
    yield


def _run_streams(streams, k_ref, s_scr, p_scr, causal, on_done, bounded):
    def stages(st):
        if bounded:
            return [_bounded_phase(st, k_ref, s_scr, p_scr, causal), _value_phase(st, p_scr)]
        return [_score_phase(st, k_ref, s_scr, causal), _exp_phase(st, s_scr, p_scr), _value_phase(st, p_scr)]

    depth = 2 if bounded else 3
    staged = [stages(st) for st in streams]
    for tick in range(len(streams) + depth - 1):
        live = [staged[tick - d][d] for d in range(depth) if 0 <= tick - d < len(streams)]
        while live:
            for g in list(live):
                if next(g, _DONE) is _DONE:
                    live.remove(g)
        if tick >= depth - 1:
            on_done(streams[tick - depth + 1])


_DONE = object()


def _attention(make_streams, bounds, k_ref, stat_ref, s_scr, p_scr, on_done):
    nq = bounds.shape[0] // 2
    worst = jnp.maximum(bounds[nq - 1:nq, :], bounds[2 * nq - 1:2 * nq, :])
    v_abs = jnp.max(functools.reduce(jnp.maximum, [stat_ref[0, 2 * j + 1:2 * j + 2, :] for j in range(nq)]))
    safe = jnp.logical_and(jnp.max(worst) <= SAFE_BOUND, v_abs <= VALUE_LIMIT)

    @pl.when(safe)
    def _():
        _run_streams(make_streams(), k_ref, s_scr, p_scr, _causal_mask(), on_done, bounded=True)

    @pl.when(jnp.logical_not(safe))
    def _():
        _run_streams(make_streams(), k_ref, s_scr, p_scr, _causal_mask(), on_done, bounded=False)


def _diff_body(q_ref, k_ref, v_ref, stat_ref, lam_ref, g_ref, o_ref, vt_scr, s_scr, p_scr, *, lambda_init):
    lam = lam_ref[...]
    lam_val = (jnp.exp(jnp.sum(lam[0:1] * lam[1:2], keepdims=True))
               - jnp.exp(jnp.sum(lam[2:3] * lam[3:4], keepdims=True)) + lambda_init)
    gain = g_ref[...] * (1.0 - lambda_init)
    nq = q_ref.shape[0] // ATT_BLOCK
    bounds = _score_bounds(q_ref, stat_ref)
    first = _channel_is_first(LANES)

    def build(i, mi, qts):
        if i not in qts:
            qts[i] = q_ref[_blk(i), :].astype(F32).T
        qm = jnp.where(first, qts[i], 0.0) if mi == 0 else jnp.where(first, 0.0, qts[i])
        bound = bounds[mi * nq + i:mi * nq + i + 1, _blk(i)] - SCORE_SHIFT
        return _Stream(i, qm.astype(BF16), None, bound, vt_scr, 2 * (i % 2) + mi, mi)

    head_qts = {}
    head = build(nq - 1, 0, head_qts)
    _prestore_scores(head, k_ref, s_scr)

    def make_streams():
        _transpose_values(v_ref, vt_scr)
        qts = dict(head_qts)
        order = [(i, mi) for i in reversed(range(nq)) for mi in range(2)]
        return [head] + [build(i, mi, qts) for i, mi in order[1:]]

    outs = {}

    def on_done(st):
        outs[st.tag] = st.acc * (1.0 / st.l)
        if st.tag == 1:
            ot = outs[0] - lam_val * outs[1]
            o_ref[_blk(st.i), :] = _rms(ot.T, gain).astype(BF16)

    _attention(make_streams, bounds, k_ref, stat_ref, s_scr, p_scr, on_done)


def _stat_spec(stat):
    return pl.BlockSpec((1, stat.shape[1], LANES), lambda b, h: (b, 0, h))


def _diff_attention(dq, dk, dv, stat, lam, subln, layer, lambda_init, seq):
    t = dq.shape[0]
    blk = pl.BlockSpec((seq, LANES), lambda b, h: (b, h))
    return pl.pallas_call(
        functools.partial(_diff_body, lambda_init=lambda_init),
        grid=(t // seq, DIFF_HEADS),
        in_specs=[blk, blk, blk, _stat_spec(stat), _layer(lam, layer), _layer(subln, layer)],
        out_specs=blk,
        out_shape=jax.ShapeDtypeStruct((t, DIFF_WIDTH), BF16),
        scratch_shapes=[pltpu.VMEM((LANES, seq), BF16), pltpu.VMEM((4, seq, ATT_BLOCK), F32),
                        pltpu.VMEM((4, seq, ATT_BLOCK), BF16)],
        compiler_params=pltpu.CompilerParams(
            dimension_semantics=("parallel", "parallel"), vmem_limit_bytes=VMEM_LIMIT),
        name="diff_attn",
    )(dq, dk, dv, stat, lam, subln)


def _moba_allow(gate, own, n_sel):
    nb = gate.shape[0]
    blk = lax.broadcasted_iota(jnp.int32, (nb, 1), 0)
    rank = jnp.zeros(gate.shape, jnp.int32)
    for other in range(own):
        g_other = gate[other:other + 1, :]
        beats = (g_other > gate) | ((g_other == gate) & (other < blk))
        rank = rank + jnp.where(beats, 1, 0)
    return jnp.where((blk < own) & (rank < n_sel), 1.0, 0.0)


def _moba_body(q_ref, k_ref, v_ref, stat_ref, kmean_ref, o_ref, vt_scr, s_scr, p_scr):
    nq = q_ref.shape[0] // ATT_BLOCK
    n_sel = min(MOBA_TOPK, nq - 1)
    bounds = _score_bounds(q_ref, stat_ref)
    first = _channel_is_first(LANES)
    kmean = kmean_ref[0]

    def build(i, hd, qts):
        if i not in qts:
            qts[i] = q_ref[_blk(i), :].astype(F32).T
        qh = jnp.where(first, qts[i], 0.0) if hd == 0 else jnp.where(first, 0.0, qts[i])
        allow = None
        if i > n_sel:
            gate = jnp.dot(kmean, qh, preferred_element_type=F32, precision=lax.Precision.HIGHEST)
            allow = _moba_allow(gate, i, n_sel)
        bound = bounds[hd * nq + i:hd * nq + i + 1, _blk(i)] - SCORE_SHIFT
        vt = vt_scr.at[hd * HEAD_DIM:(hd + 1) * HEAD_DIM, :]
        return _Stream(i, qh.astype(BF16), allow, bound, vt, 2 * (i % 2) + hd, hd)

    head_qts = {}
    head = build(nq - 1, 0, head_qts)
    _prestore_scores(head, k_ref, s_scr)

    def make_streams():
        _transpose_values(v_ref, vt_scr)
        qts = dict(head_qts)
        order = [(i, hd) for i in reversed(range(nq)) for hd in range(2)]
        return [head] + [build(i, hd, qts) for i, hd in order[1:]]

    outs = {}

    def on_done(st):
        outs[st.tag] = st.acc * (1.0 / st.l)
        if st.tag == 1:
            o_ref[_blk(st.i), :] = jnp.concatenate([outs[0], outs[1]], axis=0).T.astype(BF16)

    _attention(make_streams, bounds, k_ref, stat_ref, s_scr, p_scr, on_done)


def _moba_attention(mq, mk, mv, stat, kmean, seq):
    t = mq.shape[0]
    nq = seq // ATT_BLOCK
    blk = pl.BlockSpec((seq, LANES), lambda b, p: (b, p))
    km = pl.BlockSpec((1, nq, LANES), lambda b, p: (b, 0, p))
    return pl.pallas_call(
        _moba_body,
        grid=(t // seq, MOBA_WIDTH // LANES),
        in_specs=[blk, blk, blk, _stat_spec(stat), km],
        out_specs=blk,
        out_shape=jax.ShapeDtypeStruct((t, MOBA_WIDTH), BF16),
        scratch_shapes=[pltpu.VMEM((LANES, seq), BF16), pltpu.VMEM((4, seq, ATT_BLOCK), F32),
                        pltpu.VMEM((4, seq, ATT_BLOCK), BF16)],
        compiler_params=pltpu.CompilerParams(
            dimension_semantics=("parallel", "parallel"), vmem_limit_bytes=VMEM_LIMIT),
        name="moba_attn",
    )(mq, mk, mv, stat, kmean)


def _inproj_weights(w):
    depth, d, _ = w.shape
    q_scale = HEAD_DIM ** -0.5 * math.log2(math.e)

    def pair_interleave(sec):
        width = sec.shape[-1]
        return sec.reshape(depth, d, width // LANES, 2, 2, QUARTER).swapaxes(3, 4).reshape(depth, d, width)

    parts, base = [], 0
    for width, rotary, scale in ((POOL_WIDTH, False, None),
                                 (DIFF_WIDTH, True, q_scale), (DIFF_WIDTH, True, None), (DIFF_WIDTH, False, None),
                                 (MOBA_WIDTH, True, q_scale), (MOBA_WIDTH, True, None), (MOBA_WIDTH, False, None)):
        sec = w[:, :, base:base + width]
        if rotary:
            sec = pair_interleave(sec)
        if scale is not None:
            sec = sec * scale
        parts.append(sec)
        base += width
    return jnp.concatenate(parts, axis=-1).astype(BF16)


def _rope_tables(seq):
    inv = ROPE_THETA ** (-jnp.arange(0, HEAD_DIM, 2, dtype=F32) / HEAD_DIM)
    ang = jnp.arange(seq, dtype=F32)[:, None] * inv[None, :]
    cos, sin = jnp.cos(ang), jnp.sin(ang)
    return jnp.tile(cos, (1, 4)), jnp.concatenate([-sin, -sin, sin, sin], axis=1)


def kernel(x, ffn1_norm, ffn1_w_in, ffn1_w_out, mix_norm, mix_w_in, mix_w_out, pool_w, pool_scale,
           diff_lambda, diff_subln, ffn2_norm, ffn2_w_in, ffn2_w_out, final_norm):
    batch, seq, d = x.shape
    depth = ffn1_norm.shape[0]
    t = batch * seq
    assert seq % ROW_TILE == 0 and ROW_TILE % MOBA_BLOCK == 0 and ffn1_w_out.shape[1] % FF_CHUNK == 0

    cos, sin = _rope_tables(seq)
    gf = final_norm.reshape(1, d)
    g1, gm, g2 = (g.reshape(depth, 1, d) for g in (ffn1_norm, mix_norm, ffn2_norm))
    w1_in, w1_out, w2_in, w2_out, w_mix_out = (
        w.astype(BF16) for w in (ffn1_w_in, ffn1_w_out, ffn2_w_in, ffn2_w_out, mix_w_out))
    w_mix_in = _inproj_weights(mix_w_in)
    groups = len(POOL_WINDOWS)
    pool_bd = jnp.einsum("lgcd,gh->lgchd", pool_w, jnp.eye(groups, dtype=F32)).reshape(depth, POOL_WIDTH, POOL_WIDTH)
    pool_sc = pool_scale.reshape(depth, 1, POOL_WIDTH)
    subln = diff_subln.reshape(depth, 1, LANES)

    xt = x.reshape(t, d)
    for l in range(depth):
        lambda_init = 0.8 - 0.6 * math.exp(-0.3 * l)
        xt = _ffn(xt, g1, w1_in, w1_out, gf, l, False)
        u, dq, dk, dv, mq, mk, mv, kmean, dstat, mstat = _inproj(xt, gm, w_mix_in, cos, sin, l, seq)
        nblk = seq // ATT_BLOCK
        dstat = dstat.reshape(batch, 2 * nblk, DIFF_WIDTH)
        mstat = mstat.reshape(batch, 2 * nblk, MOBA_WIDTH)
        ya = _pool(u, pool_bd, pool_sc, l, seq)
        yb = _diff_attention(dq, dk, dv, dstat, diff_lambda, subln, l, lambda_init, seq)
        yc = _moba_attention(mq, mk, mv, mstat, kmean.reshape(batch, nblk, MOBA_WIDTH), seq)
        xt = _ffn(xt, g2, w2_in, w2_out, gf, l, l == depth - 1, mix=(ya, yb, yc, w_mix_out))
    return xt.reshape(batch, seq, d)
```

```python
import functools
import math

import jax
import jax.numpy as jnp
from jax import lax
from jax.experimental import pallas as pl
from jax.experimental.pallas import tpu as pltpu

HEAD_DIM = 64
POOL_WINDOWS = (2, 4, 8, 16)
POOL_GROUP_DIM = 64
POOL_WIDTH = 256
DIFF_WIDTH = 512
DIFF_HEADS = 4
MOBA_WIDTH = 256
MOBA_BLOCK = 256
MOBA_TOPK = 3
ROPE_THETA = 10000.0
NORM_EPS = 1e-6
NEG_INF = -1e30

LANES = 128
BF16_ROWS = 16
MXU_COLS = 256
assert POOL_WIDTH == MXU_COLS and MOBA_WIDTH == MXU_COLS and DIFF_WIDTH % MXU_COLS == 0
QUARTER = HEAD_DIM // 2
ATT_BLOCK = 256
ROW_TILE = 1024
FF_CHUNK = 256
SCORE_SHIFT = 40.0
SAFE_BOUND = 70.0
VALUE_LIMIT = 2.0 ** 60
VMEM_LIMIT = 56 * 1024 * 1024

F32 = jnp.float32
BF16 = jnp.bfloat16


def _dot(a, b):
    return jnp.dot(a, b, preferred_element_type=F32)


def _rms(x, g):
    return x * lax.rsqrt(jnp.mean(x * x, axis=-1, keepdims=True) + NORM_EPS) * g


def _resident(shape):
    nd = len(shape)
    return pl.BlockSpec(shape, lambda *_: (0,) * nd, pipeline_mode=pl.Buffered(1))


def _layer(arr, layer):
    tail = (0,) * (arr.ndim - 1)
    return pl.BlockSpec((None,) + arr.shape[1:], lambda *_: (layer,) + tail, pipeline_mode=pl.Buffered(1))


class _SideCasts:
    def __init__(self, weights, layer, steps, step_of):
        self.views, self.in_specs, self.out_shape, self.out_specs, self.shapes = [], [], [], [], []
        for w in weights:
            depth, a, b = w.shape
            cols = next(c for c in (1024, 512, 256, 128) if (a * b) % (c * steps * BF16_ROWS) == 0)
            rows = a * b // cols
            slab = rows // steps
            self.views.append(w.reshape(depth, rows, cols))
            self.in_specs.append(pl.BlockSpec((None, slab, cols), lambda *g: (layer, step_of(*g), 0)))
            self.out_shape.append(jax.ShapeDtypeStruct((rows, cols), BF16))
            self.out_specs.append(pl.BlockSpec((slab, cols), lambda *g: (step_of(*g), 0)))
            self.shapes.append((1, a, b))

    def __len__(self):
        return len(self.views)

    def restore(self, outs):
        return [o.reshape(s) for o, s in zip(outs, self.shapes)]


def _cast_slabs(in_refs, out_refs):
    for src, dst in zip(in_refs, out_refs):
        dst[...] = src[...].astype(BF16)


def _ffn_body(x_ref, g_ref, win_ref, wout_ref, gf_ref, *rest, d_ff, final_norm, with_mix):
    if with_mix:
        ya_ref, yb_ref, yc_ref, wmix_ref, o_ref, u_scr = rest
        lo_b, lo_c = POOL_WIDTH, POOL_WIDTH + DIFF_WIDTH
        x = (x_ref[...] + _dot(ya_ref[...], wmix_ref[0:lo_b, :]) + _dot(yb_ref[...], wmix_ref[lo_b:lo_c, :])
             + _dot(yc_ref[...], wmix_ref[lo_c:, :]))
    else:
        o_ref, u_scr = rest
        x = x_ref[...]
    h = _rms(x, g_ref[...]).astype(BF16)
    for c in range(d_ff // FF_CHUNK):
        lo = c * FF_CHUNK
        a = _dot(h, win_ref[:, lo:lo + FF_CHUNK])
        b = _dot(h, win_ref[:, d_ff + lo:d_ff + lo + FF_CHUNK])
        u_scr[:, lo:lo + FF_CHUNK] = (a / (1.0 + jnp.exp(-a)) * b).astype(BF16)
    y = x + 0.5 * _dot(u_scr[...], wout_ref[...])
    if final_norm:
        y = _rms(y, gf_ref[...])
    o_ref[...] = y


def _ffn(x, g, w_in, w_out, gf, g_layer, w_layer, final_norm, mix=None):
    t, d = x.shape
    d_ff = w_out.shape[1]

    def rows(width):
        return pl.BlockSpec((ROW_TILE, width), lambda i: (i, 0))

    args = [x, g, w_in, w_out, gf]
    in_specs = [rows(d), _layer(g, g_layer), _layer(w_in, w_layer), _layer(w_out, w_layer), _resident((1, d))]
    if mix is not None:
        args.extend(mix)
        in_specs.extend([rows(POOL_WIDTH), rows(DIFF_WIDTH), rows(MOBA_WIDTH), _layer(mix[3], w_layer)])
    return pl.pallas_call(
        functools.partial(_ffn_body, d_ff=d_ff, final_norm=final_norm, with_mix=mix is not None),
        grid=(t // ROW_TILE,),
        in_specs=in_specs,
        out_specs=rows(d),
        out_shape=jax.ShapeDtypeStruct((t, d), F32),
        scratch_shapes=[pltpu.VMEM((ROW_TILE, d_ff), BF16)],
        compiler_params=pltpu.CompilerParams(dimension_semantics=("parallel",), vmem_limit_bytes=VMEM_LIMIT),
        name="mix_ffn" if mix is not None else "ffn",
    )(*args)


_INPROJ_INPUTS = 5
_INPROJ_OUTPUTS = 10


def _inproj_body(*refs):
    n_cast = (len(refs) - _INPROJ_INPUTS - _INPROJ_OUTPUTS) // 2
    x_ref, g_ref, w_ref, cos_ref, sin_ref = refs[:_INPROJ_INPUTS]
    cast_in = refs[_INPROJ_INPUTS:_INPROJ_INPUTS + n_cast]
    outs = refs[_INPROJ_INPUTS + n_cast:]
    u_ref, dq_ref, dk_ref, dv_ref, mq_ref, mk_ref, mv_ref, kmean_ref, dstat_ref, mstat_ref = outs[:_INPROJ_OUTPUTS]
    _cast_slabs(cast_in, outs[_INPROJ_OUTPUTS:])

    h = _rms(x_ref[...], g_ref[...]).astype(BF16)
    cos = cos_ref[...]
    sin = sin_ref[...]

    def proj(lo):
        return _dot(h, w_ref[:, lo:lo + MXU_COLS])

    def rope(z):
        halves = [z[:, c * LANES:(c + 1) * LANES] for c in range(MXU_COLS // LANES)]
        return jnp.concatenate([zc * cos + pltpu.roll(zc, 2 * QUARTER, axis=1) * sin for zc in halves], axis=1)

    def put_absmax(stat_ref, row, z, lo):
        for r in range(ROW_TILE // ATT_BLOCK):
            m = jnp.max(jnp.abs(z[r * ATT_BLOCK:(r + 1) * ATT_BLOCK].astype(F32)), axis=0, keepdims=True)
            stat_ref[r, row:row + 1, lo:lo + MXU_COLS] = m

    u_ref[...] = proj(0)
    base = POOL_WIDTH
    for c in range(DIFF_WIDTH // MXU_COLS):
        dq_ref[:, c * MXU_COLS:(c + 1) * MXU_COLS] = rope(proj(base + c * MXU_COLS)).astype(BF16)
    base += DIFF_WIDTH
    for ref, row, rotary in ((dk_ref, 0, True), (dv_ref, 1, False)):
        for c in range(DIFF_WIDTH // MXU_COLS):
            z = proj(base + c * MXU_COLS)
            z = (rope(z) if rotary else z).astype(BF16)
            ref[:, c * MXU_COLS:(c + 1) * MXU_COLS] = z
            put_absmax(dstat_ref, row, z, c * MXU_COLS)
        base += DIFF_WIDTH
    mq_ref[...] = rope(proj(base)).astype(BF16)
    base += MOBA_WIDTH
    k = rope(proj(base))
    mk_ref[...] = k.astype(BF16)
    put_absmax(mstat_ref, 0, k.astype(BF16), 0)
    for r in range(ROW_TILE // MOBA_BLOCK):
        kmean_ref[r] = jnp.mean(k[r * MOBA_BLOCK:(r + 1) * MOBA_BLOCK], axis=0, keepdims=True)
    base += MOBA_WIDTH
    v = proj(base).astype(BF16)
    mv_ref[...] = v
    put_absmax(mstat_ref, 1, v, 0)


def _inproj(x, g, w, cos, sin, layer, seq, cast_weights):
    t, d = x.shape
    nt = t // ROW_TILE
    tiles_per_seq = seq // ROW_TILE
    blocks_per_tile = ROW_TILE // MOBA_BLOCK
    casts = _SideCasts(cast_weights, layer, nt, lambda i: i)

    def rows(width):
        return pl.BlockSpec((ROW_TILE, width), lambda i: (i, 0))

    table = pl.BlockSpec((ROW_TILE, LANES), lambda i: (i % tiles_per_seq, 0))
    widths = (POOL_WIDTH, DIFF_WIDTH, DIFF_WIDTH, DIFF_WIDTH, MOBA_WIDTH, MOBA_WIDTH, MOBA_WIDTH)
    dtypes = (F32, BF16, BF16, BF16, BF16, BF16, BF16)
    out_shape = [jax.ShapeDtypeStruct((t, wd), dt) for wd, dt in zip(widths, dtypes)]
    out_shape.append(jax.ShapeDtypeStruct((t // MOBA_BLOCK, 1, MOBA_WIDTH), F32))
    out_specs = [rows(wd) for wd in widths]
    out_specs.append(pl.BlockSpec((blocks_per_tile, 1, MOBA_WIDTH), lambda i: (i, 0, 0)))
    for wd in (DIFF_WIDTH, MOBA_WIDTH):
        out_shape.append(jax.ShapeDtypeStruct((t // ATT_BLOCK, 2, wd), F32))
        out_specs.append(pl.BlockSpec((blocks_per_tile, 2, wd), lambda i: (i, 0, 0)))
    assert len(out_shape) == _INPROJ_OUTPUTS
    outs = pl.pallas_call(
        _inproj_body,
        grid=(nt,),
        in_specs=[rows(d), _layer(g, layer), _layer(w, layer), table, table] + casts.in_specs,
        out_specs=out_specs + casts.out_specs,
        out_shape=out_shape + casts.out_shape,
        compiler_params=pltpu.CompilerParams(dimension_semantics=("parallel",), vmem_limit_bytes=VMEM_LIMIT),
        name="inproj",
    )(x, g, w, cos, sin, *casts.views)
    return list(outs[:_INPROJ_OUTPUTS]) + casts.restore(outs[_INPROJ_OUTPUTS:])


def _pool_body(u_ref, w_ref, scale_ref, o_ref):
    s = u_ref.shape[0]
    row = lax.broadcasted_iota(jnp.int32, (s, 1), 0)
    groups_per_tile = LANES // POOL_GROUP_DIM
    lane_group = lax.broadcasted_iota(jnp.int32, (1, LANES), 1) // POOL_GROUP_DIM
    tiles = []
    for t in range(POOL_WIDTH // LANES):
        u = u_ref[:, t * LANES:(t + 1) * LANES]
        window_sum, shift, pooled = u, 1, None
        for gi in range(groups_per_tile):
            wdw = POOL_WINDOWS[t * groups_per_tile + gi]
            while shift < wdw:
                shifted = jnp.where(row >= shift, pltpu.roll(window_sum, shift, axis=0), 0.0)
                window_sum = window_sum + shifted
                shift *= 2
            mean = window_sum * (1.0 / jnp.minimum(row + 1, wdw).astype(F32))
            pooled = mean if pooled is None else jnp.where(lane_group == gi, mean, pooled)
        tiles.append(pooled - u)
    mixed = _dot(jnp.concatenate(tiles, axis=1), w_ref[...])
    o_ref[...] = (mixed * scale_ref[...]).astype(BF16)


def _pool(u, w_blockdiag, scale, layer, seq):
    t = u.shape[0]
    blk = pl.BlockSpec((seq, POOL_WIDTH), lambda b: (b, 0))
    return pl.pallas_call(
        _pool_body,
        grid=(t // seq,),
        in_specs=[blk, _layer(w_blockdiag, layer), _layer(scale, layer)],
        out_specs=blk,
        out_shape=jax.ShapeDtypeStruct((t, POOL_WIDTH), BF16),
        compiler_params=pltpu.CompilerParams(dimension_semantics=("parallel",), vmem_limit_bytes=VMEM_LIMIT),
        name="pool",
    )(u, w_blockdiag, scale)


def _channel_is_first(n):
    ch = lax.broadcasted_iota(jnp.int32, (n, 1), 0)
    return (ch // QUARTER) % 2 == 0


def _blk(j):
    return slice(j * ATT_BLOCK, (j + 1) * ATT_BLOCK)


def _transpose_values(v_ref, vt_scr):
    for j in range(v_ref.shape[0] // ATT_BLOCK):
        vt_scr[:, _blk(j)] = v_ref[_blk(j), :].astype(F32).T.astype(BF16)


def _causal_mask():
    key = lax.broadcasted_iota(jnp.int32, (ATT_BLOCK, ATT_BLOCK), 0)
    qry = lax.broadcasted_iota(jnp.int32, (ATT_BLOCK, ATT_BLOCK), 1)
    return key <= qry


def _score_bounds(q_ref, stat_ref):
    nq = stat_ref.shape[1] // 2
    rows, run = [], None
    for j in range(nq):
        m = stat_ref[0, 2 * j:2 * j + 1, :]
        run = m if run is None else jnp.maximum(run, m)
        rows.append(run)
    kabs = jnp.concatenate(rows, axis=0)
    lane_first = (lax.broadcasted_iota(jnp.int32, (1, LANES), 1) // QUARTER) % 2 == 0
    ka = jnp.concatenate([jnp.where(lane_first, kabs, 0.0), jnp.where(lane_first, 0.0, kabs)], axis=0)
    b = lax.dot_general(ka.astype(BF16), jnp.abs(q_ref[...]), (((1,), (1,)), ((), ())), preferred_element_type=F32)
    return b * (1.0 + 2.0 ** -10) + 2.0 ** -10


class _Stream:
    def __init__(self, i, qt, allow, bound, vt, slot, tag):
        self.i, self.qt, self.allow, self.bound, self.vt, self.slot, self.tag = i, qt, allow, bound, vt, slot, tag
        self.mx = self.l = self.acc = None
        self.prestored = False


def _masked_scores(st, j, k_ref, s_scr, causal):
    if st.prestored:
        return s_scr[st.slot, _blk(j), :]
    s = _dot(k_ref[_blk(j), :], st.qt)
    if j == st.i:
        s = jnp.where(causal, s, NEG_INF)
    elif st.allow is not None:
        s = jnp.where(st.allow[j:j + 1, :] > 0.5, s, NEG_INF)
    return s


def _prestore_scores(st, k_ref, s_scr):
    causal = _causal_mask()
    for j in range(st.i + 1):
        s_scr[st.slot, _blk(j), :] = _masked_scores(st, j, k_ref, s_scr, causal)
    st.prestored = True


def _score_phase(st, k_ref, s_scr, causal):
    mx = None
    for j in range(st.i + 1):
        s = _masked_scores(st, j, k_ref, s_scr, causal)
        if not st.prestored:
            s_scr[st.slot, _blk(j), :] = s
        bm = jnp.max(s, axis=0, keepdims=True)
        mx = bm if mx is None else jnp.maximum(mx, bm)
        yield
    st.mx = mx


def _exp_phase(st, s_scr, p_scr):
    for j in range(st.i + 1):
        _emit_probabilities(st, j, jnp.exp2(s_scr[st.slot, _blk(j), :] - st.mx), p_scr)
        yield


def _bounded_phase(st, k_ref, s_scr, p_scr, causal):
    for j in range(st.i + 1):
        _emit_probabilities(st, j, jnp.exp2(_masked_scores(st, j, k_ref, s_scr, causal) - st.bound), p_scr)
        yield


def _emit_probabilities(st, j, p, p_scr):
    p_scr[st.slot, _blk(j), :] = p.astype(BF16)
    bl = jnp.sum(p, axis=0, keepdims=True)
    st.l = bl if j == 0 else st.l + bl


def _value_phase(st, p_scr):
    n = (st.i + 1) * ATT_BLOCK
    st.acc = _dot(st.vt[:, 0:n], p_scr[st.slot, 0:n, :])
    yield


def _run_streams(streams, k_ref, s_scr, p_scr, causal, on_done, bounded):
    def stages(st):
        if bounded:
            return [_bounded_phase(st, k_ref, s_scr, p_scr, causal), _value_phase(st, p_scr)]
        return [_score_phase(st, k_ref, s_scr, causal), _exp_phase(st, s_scr, p_scr), _value_phase(st, p_scr)]

    depth = 2 if bounded else 3
    staged = [stages(st) for st in streams]
    for tick in range(len(streams) + depth - 1):
        live = [staged[tick - d][d] for d in range(depth) if 0 <= tick - d < len(streams)]
        while live:
            for g in list(live):
                if next(g, _DONE) is _DONE:
                    live.remove(g)
        if tick >= depth - 1:
            on_done(streams[tick - depth + 1])


_DONE = object()


def _attention(make_streams, bounds, k_ref, stat_ref, s_scr, p_scr, on_done):
    nq = bounds.shape[0] // 2
    worst = jnp.maximum(bounds[nq - 1:nq, :], bounds[2 * nq - 1:2 * nq, :])
    v_abs = jnp.max(functools.reduce(jnp.maximum, [stat_ref[0, 2 * j + 1:2 * j + 2, :] for j in range(nq)]))
    safe = jnp.logical_and(jnp.max(worst) <= SAFE_BOUND, v_abs <= VALUE_LIMIT)

    @pl.when(safe)
    def _():
        _run_streams(make_streams(), k_ref, s_scr, p_scr, _causal_mask(), on_done, bounded=True)

    @pl.when(jnp.logical_not(safe))
    def _():
        _run_streams(make_streams(), k_ref, s_scr, p_scr, _causal_mask(), on_done, bounded=False)


def _diff_body(q_ref, k_ref, v_ref, stat_ref, lam_ref, g_ref, o_ref, vt_scr, s_scr, p_scr, *, lambda_init):
    lam = lam_ref[...]
    lam_val = (jnp.exp(jnp.sum(lam[0:1] * lam[1:2], keepdims=True))
               - jnp.exp(jnp.sum(lam[2:3] * lam[3:4], keepdims=True)) + lambda_init)
    gain = g_ref[...] * (1.0 - lambda_init)
    nq = q_ref.shape[0] // ATT_BLOCK
    bounds = _score_bounds(q_ref, stat_ref)
    first = _channel_is_first(LANES)

    def build(i, mi, qts):
        if i not in qts:
            qts[i] = q_ref[_blk(i), :].astype(F32).T
        qm = jnp.where(first, qts[i], 0.0) if mi == 0 else jnp.where(first, 0.0, qts[i])
        bound = bounds[mi * nq + i:mi * nq + i + 1, _blk(i)] - SCORE_SHIFT
        return _Stream(i, qm.astype(BF16), None, bound, vt_scr, 2 * (i % 2) + mi, mi)

    head_qts = {}
    head = build(nq - 1, 0, head_qts)
    _prestore_scores(head, k_ref, s_scr)

    def make_streams():
        _transpose_values(v_ref, vt_scr)
        qts = dict(head_qts)
        order = [(i, mi) for i in reversed(range(nq)) for mi in range(2)]
        return [head] + [build(i, mi, qts) for i, mi in order[1:]]

    outs = {}

    def on_done(st):
        outs[st.tag] = st.acc * (1.0 / st.l)
        if st.tag == 1:
            ot = outs[0] - lam_val * outs[1]
            o_ref[_blk(st.i), :] = _rms(ot.T, gain).astype(BF16)

    _attention(make_streams, bounds, k_ref, stat_ref, s_scr, p_scr, on_done)


def _stat_spec(stat):
    return pl.BlockSpec((1, stat.shape[1], LANES), lambda b, h: (b, 0, h))


def _diff_attention(dq, dk, dv, stat, lam, subln, layer, lambda_init, seq):
    t = dq.shape[0]
    blk = pl.BlockSpec((seq, LANES), lambda b, h: (b, h))
    return pl.pallas_call(
        functools.partial(_diff_body, lambda_init=lambda_init),
        grid=(t // seq, DIFF_HEADS),
        in_specs=[blk, blk, blk, _stat_spec(stat), _layer(lam, layer), _layer(subln, layer)],
        out_specs=blk,
        out_shape=jax.ShapeDtypeStruct((t, DIFF_WIDTH), BF16),
        scratch_shapes=[pltpu.VMEM((LANES, seq), BF16), pltpu.VMEM((4, seq, ATT_BLOCK), F32),
                        pltpu.VMEM((4, seq, ATT_BLOCK), BF16)],
        compiler_params=pltpu.CompilerParams(
            dimension_semantics=("parallel", "parallel"), vmem_limit_bytes=VMEM_LIMIT),
        name="diff_attn",
    )(dq, dk, dv, stat, lam, subln)


_MOBA_INPUTS = 5
_MOBA_SCRATCH = 3


def _moba_allow(gate, own, n_sel):
    nb = gate.shape[0]
    blk = lax.broadcasted_iota(jnp.int32, (nb, 1), 0)
    rank = jnp.zeros(gate.shape, jnp.int32)
    for other in range(own):
        g_other = gate[other:other + 1, :]
        beats = (g_other > gate) | ((g_other == gate) & (other < blk))
        rank = rank + jnp.where(beats, 1, 0)
    return jnp.where((blk < own) & (rank < n_sel), 1.0, 0.0)


def _moba_body(*refs):
    n_cast = (len(refs) - _MOBA_INPUTS - 1 - _MOBA_SCRATCH) // 2
    q_ref, k_ref, v_ref, stat_ref, kmean_ref = refs[:_MOBA_INPUTS]
    cast_in = refs[_MOBA_INPUTS:_MOBA_INPUTS + n_cast]
    o_ref = refs[_MOBA_INPUTS + n_cast]
    cast_out = refs[_MOBA_INPUTS + n_cast + 1:_MOBA_INPUTS + 2 * n_cast + 1]
    vt_scr, s_scr, p_scr = refs[-_MOBA_SCRATCH:]
    _cast_slabs(cast_in, cast_out)

    nq = q_ref.shape[0] // ATT_BLOCK
    n_sel = min(MOBA_TOPK, nq - 1)
    bounds = _score_bounds(q_ref, stat_ref)
    first = _channel_is_first(LANES)
    kmean = kmean_ref[0]

    def build(i, hd, qts):
        if i not in qts:
            qts[i] = q_ref[_blk(i), :].astype(F32).T
        qh = jnp.where(first, qts[i], 0.0) if hd == 0 else jnp.where(first, 0.0, qts[i])
        allow = None
        if i > n_sel:
            gate = jnp.dot(kmean, qh, preferred_element_type=F32, precision=lax.Precision.HIGHEST)
            allow = _moba_allow(gate, i, n_sel)
        bound = bounds[hd * nq + i:hd * nq + i + 1, _blk(i)] - SCORE_SHIFT
        vt = vt_scr.at[hd * HEAD_DIM:(hd + 1) * HEAD_DIM, :]
        return _Stream(i, qh.astype(BF16), allow, bound, vt, 2 * (i % 2) + hd, hd)

    head_qts = {}
    head = build(nq - 1, 0, head_qts)
    _prestore_scores(head, k_ref, s_scr)

    def make_streams():
        _transpose_values(v_ref, vt_scr)
        qts = dict(head_qts)
        order = [(i, hd) for i in reversed(range(nq)) for hd in range(2)]
        return [head] + [build(i, hd, qts) for i, hd in order[1:]]

    outs = {}

    def on_done(st):
        outs[st.tag] = st.acc * (1.0 / st.l)
        if st.tag == 1:
            o_ref[_blk(st.i), :] = jnp.concatenate([outs[0], outs[1]], axis=0).T.astype(BF16)

    _attention(make_streams, bounds, k_ref, stat_ref, s_scr, p_scr, on_done)


def _moba_attention(mq, mk, mv, stat, kmean, seq, cast_weights, cast_layer):
    t = mq.shape[0]
    nq = seq // ATT_BLOCK
    pairs = MOBA_WIDTH // LANES
    casts = _SideCasts(cast_weights, cast_layer, (t // seq) * pairs, lambda b, p: b * pairs + p)
    blk = pl.BlockSpec((seq, LANES), lambda b, p: (b, p))
    km = pl.BlockSpec((1, nq, LANES), lambda b, p: (b, 0, p))
    outs = pl.pallas_call(
        _moba_body,
        grid=(t // seq, pairs),
        in_specs=[blk, blk, blk, _stat_spec(stat), km] + casts.in_specs,
        out_specs=[blk] + casts.out_specs,
        out_shape=[jax.ShapeDtypeStruct((t, MOBA_WIDTH), BF16)] + casts.out_shape,
        scratch_shapes=[pltpu.VMEM((LANES, seq), BF16), pltpu.VMEM((4, seq, ATT_BLOCK), F32),
                        pltpu.VMEM((4, seq, ATT_BLOCK), BF16)],
        compiler_params=pltpu.CompilerParams(
            dimension_semantics=("parallel", "parallel"), vmem_limit_bytes=VMEM_LIMIT),
        name="moba_attn",
    )(mq, mk, mv, stat, kmean, *casts.views)
    return [outs[0]] + casts.restore(outs[1:])


def _inproj_weights(w):
    depth, d, _ = w.shape
    q_scale = HEAD_DIM ** -0.5 * math.log2(math.e)

    def pair_interleave(sec):
        width = sec.shape[-1]
        return sec.reshape(depth, d, width // LANES, 2, 2, QUARTER).swapaxes(3, 4).reshape(depth, d, width)

    parts, base = [], 0
    for width, rotary, scale in ((POOL_WIDTH, False, None),
                                 (DIFF_WIDTH, True, q_scale), (DIFF_WIDTH, True, None), (DIFF_WIDTH, False, None),
                                 (MOBA_WIDTH, True, q_scale), (MOBA_WIDTH, True, None), (MOBA_WIDTH, False, None)):
        sec = w[:, :, base:base + width]
        if rotary:
            sec = pair_interleave(sec)
        if scale is not None:
            sec = sec * scale
        parts.append(sec)
        base += width
    return jnp.concatenate(parts, axis=-1).astype(BF16)


def _rope_tables(seq):
    inv = ROPE_THETA ** (-jnp.arange(0, HEAD_DIM, 2, dtype=F32) / HEAD_DIM)
    ang = jnp.arange(seq, dtype=F32)[:, None] * inv[None, :]
    cos, sin = jnp.cos(ang), jnp.sin(ang)
    return jnp.tile(cos, (1, 4)), jnp.concatenate([-sin, -sin, sin, sin], axis=1)


def kernel(x, ffn1_norm, ffn1_w_in, ffn1_w_out, mix_norm, mix_w_in, mix_w_out, pool_w, pool_scale,
           diff_lambda, diff_subln, ffn2_norm, ffn2_w_in, ffn2_w_out, final_norm):
    batch, seq, d = x.shape
    depth = ffn1_norm.shape[0]
    t = batch * seq
    assert seq % ROW_TILE == 0 and ROW_TILE % MOBA_BLOCK == 0 and ffn1_w_out.shape[1] % FF_CHUNK == 0

    cos, sin = _rope_tables(seq)
    gf = final_norm.reshape(1, d)
    g1, gm, g2 = (g.reshape(depth, 1, d) for g in (ffn1_norm, mix_norm, ffn2_norm))
    w_mix_in = _inproj_weights(mix_w_in)
    groups = len(POOL_WINDOWS)
    pool_bd = jnp.einsum("lgcd,gh->lgchd", pool_w, jnp.eye(groups, dtype=F32)).reshape(depth, POOL_WIDTH, POOL_WIDTH)
    pool_sc = pool_scale.reshape(depth, 1, POOL_WIDTH)
    subln = diff_subln.reshape(depth, 1, LANES)
    nblk = seq // ATT_BLOCK

    w1_in, w1_out = ffn1_w_in[:1].astype(BF16), ffn1_w_out[:1].astype(BF16)
    xt = x.reshape(t, d)
    for l in range(depth):
        lambda_init = 0.8 - 0.6 * math.exp(-0.3 * l)
        xt = _ffn(xt, g1, w1_in, w1_out, gf, l, 0, False)
        (u, dq, dk, dv, mq, mk, mv, kmean, dstat, mstat, w2_in, w2_out, w_mix_out) = _inproj(
            xt, gm, w_mix_in, cos, sin, l, seq, [ffn2_w_in, ffn2_w_out, mix_w_out])
        dstat = dstat.reshape(batch, 2 * nblk, DIFF_WIDTH)
        mstat = mstat.reshape(batch, 2 * nblk, MOBA_WIDTH)
        ya = _pool(u, pool_bd, pool_sc, l, seq)
        yb = _diff_attention(dq, dk, dv, dstat, diff_lambda, subln, l, lambda_init, seq)
        next_ffn1 = [ffn1_w_in, ffn1_w_out] if l + 1 < depth else []
        yc, *next_w1 = _moba_attention(mq, mk, mv, mstat, kmean.reshape(batch, nblk, MOBA_WIDTH), seq,
                                       next_ffn1, l + 1)
        xt = _ffn(xt, g2, w2_in, w2_out, gf, l, 0, l == depth - 1, mix=(ya, yb, yc, w_mix_out))
        if next_w1:
            w1_in, w1_out = next_w1
    return xt.reshape(batch, seq, d)
```

```python
import functools
import math

import jax
import jax.numpy as jnp
from jax import lax
from jax.experimental import pallas as pl
from jax.experimental.pallas import tpu as pltpu

HEAD_DIM = 64
POOL_WINDOWS = (2, 4, 8, 16)
POOL_GROUP_DIM = 64
POOL_WIDTH = 256
DIFF_WIDTH = 512
DIFF_HEADS = 4
MOBA_WIDTH = 256
MOBA_BLOCK = 256
MOBA_TOPK = 3
ROPE_THETA = 10000.0
NORM_EPS = 1e-6
NEG_INF = -1e30

LANES = 128
BF16_ROWS = 16
MXU_COLS = 256
assert POOL_WIDTH == MXU_COLS and MOBA_WIDTH == MXU_COLS and DIFF_WIDTH % MXU_COLS == 0
QUARTER = HEAD_DIM // 2
ATT_BLOCK = 256
ROW_TILE = 1024
FF_CHUNK = 256
CAST_STEPS = 16
SCORE_SHIFT = 40.0
SAFE_BOUND = 70.0
VALUE_LIMIT = 2.0 ** 60
VMEM_LIMIT = 56 * 1024 * 1024

F32 = jnp.float32
BF16 = jnp.bfloat16


def _dot(a, b):
    return jnp.dot(a, b, preferred_element_type=F32)


def _rms(x, g):
    return x * lax.rsqrt(jnp.mean(x * x, axis=-1, keepdims=True) + NORM_EPS) * g


def _resident(shape):
    nd = len(shape)
    return pl.BlockSpec(shape, lambda *_: (0,) * nd, pipeline_mode=pl.Buffered(1))


def _layer(arr, layer):
    tail = (0,) * (arr.ndim - 1)
    return pl.BlockSpec((None,) + arr.shape[1:], lambda *_: (layer,) + tail, pipeline_mode=pl.Buffered(1))


class _SideCasts:
    def __init__(self, weights, layer, steps, step_of):
        self.in_specs, self.out_shape, self.out_specs = [], [], []
        for w in weights:
            _, a, b = w.shape
            col_parts = next(c for c in range(1, steps + 1) if steps % c == 0 and b % (c * LANES) == 0
                             and a % ((steps // c) * BF16_ROWS) == 0)
            slab = (a // (steps // col_parts), b // col_parts)

            def index(*g, _layer=layer, _cp=col_parts):
                step = step_of(*g)
                return (_layer, step // _cp, step % _cp)

            self.in_specs.append(pl.BlockSpec((None,) + slab, index))
            self.out_shape.append(jax.ShapeDtypeStruct((1, a, b), BF16))
            self.out_specs.append(pl.BlockSpec((None,) + slab, functools.partial(index, _layer=0)))


def _cast_slabs(in_refs, out_refs):
    for src, dst in zip(in_refs, out_refs):
        dst[...] = src[...].astype(BF16)


def _cast_body(*refs):
    _cast_slabs(refs[:len(refs) // 2], refs[len(refs) // 2:])


def _cast_layer(weights, layer):
    casts = _SideCasts(weights, layer, CAST_STEPS, lambda i: i)
    return pl.pallas_call(
        _cast_body,
        grid=(CAST_STEPS,),
        in_specs=casts.in_specs,
        out_specs=casts.out_specs,
        out_shape=casts.out_shape,
        compiler_params=pltpu.CompilerParams(dimension_semantics=("parallel",), vmem_limit_bytes=VMEM_LIMIT),
        name="cast",
    )(*weights)


def _ffn_body(x_ref, g_ref, win_ref, wout_ref, gf_ref, *rest, d_ff, final_norm, with_mix):
    if with_mix:
        ya_ref, yb_ref, yc_ref, wmix_ref, o_ref, u_scr = rest
        lo_b, lo_c = POOL_WIDTH, POOL_WIDTH + DIFF_WIDTH
        x = (x_ref[...] + _dot(ya_ref[...], wmix_ref[0:lo_b, :]) + _dot(yb_ref[...], wmix_ref[lo_b:lo_c, :])
             + _dot(yc_ref[...], wmix_ref[lo_c:, :]))
    else:
        o_ref, u_scr = rest
        x = x_ref[...]
    h = _rms(x, g_ref[...]).astype(BF16)
    for c in range(d_ff // FF_CHUNK):
        lo = c * FF_CHUNK
        a = _dot(h, win_ref[:, lo:lo + FF_CHUNK])
        b = _dot(h, win_ref[:, d_ff + lo:d_ff + lo + FF_CHUNK])
        u_scr[:, lo:lo + FF_CHUNK] = (a / (1.0 + jnp.exp(-a)) * b).astype(BF16)
    y = x + 0.5 * _dot(u_scr[...], wout_ref[...])
    if final_norm:
        y = _rms(y, gf_ref[...])
    o_ref[...] = y


def _ffn(x, g, w_in, w_out, gf, g_layer, w_layer, final_norm, mix=None):
    t, d = x.shape
    d_ff = w_out.shape[1]

    def rows(width):
        return pl.BlockSpec((ROW_TILE, width), lambda i: (i, 0))

    args = [x, g, w_in, w_out, gf]
    in_specs = [rows(d), _layer(g, g_layer), _layer(w_in, w_layer), _layer(w_out, w_layer), _resident((1, d))]
    if mix is not None:
        args.extend(mix)
        in_specs.extend([rows(POOL_WIDTH), rows(DIFF_WIDTH), rows(MOBA_WIDTH), _layer(mix[3], w_layer)])
    return pl.pallas_call(
        functools.partial(_ffn_body, d_ff=d_ff, final_norm=final_norm, with_mix=mix is not None),
        grid=(t // ROW_TILE,),
        in_specs=in_specs,
        out_specs=rows(d),
        out_shape=jax.ShapeDtypeStruct((t, d), F32),
        scratch_shapes=[pltpu.VMEM((ROW_TILE, d_ff), BF16)],
        compiler_params=pltpu.CompilerParams(dimension_semantics=("parallel",), vmem_limit_bytes=VMEM_LIMIT),
        name="mix_ffn" if mix is not None else "ffn",
    )(*args)


_INPROJ_INPUTS = 5
_INPROJ_OUTPUTS = 10


def _inproj_body(*refs):
    n_cast = (len(refs) - _INPROJ_INPUTS - _INPROJ_OUTPUTS) // 2
    x_ref, g_ref, w_ref, cos_ref, sin_ref = refs[:_INPROJ_INPUTS]
    cast_in = refs[_INPROJ_INPUTS:_INPROJ_INPUTS + n_cast]
    outs = refs[_INPROJ_INPUTS + n_cast:]
    u_ref, dq_ref, dk_ref, dv_ref, mq_ref, mk_ref, mv_ref, kmean_ref, dstat_ref, mstat_ref = outs[:_INPROJ_OUTPUTS]
    _cast_slabs(cast_in, outs[_INPROJ_OUTPUTS:])

    h = _rms(x_ref[...], g_ref[...]).astype(BF16)
    cos = cos_ref[...]
    sin = sin_ref[...]

    def proj(lo):
        return _dot(h, w_ref[:, lo:lo + MXU_COLS])

    def rope(z):
        halves = [z[:, c * LANES:(c + 1) * LANES] for c in range(MXU_COLS // LANES)]
        return jnp.concatenate([zc * cos + pltpu.roll(zc, 2 * QUARTER, axis=1) * sin for zc in halves], axis=1)

    def put_absmax(stat_ref, row, z, lo):
        for r in range(ROW_TILE // ATT_BLOCK):
            m = jnp.max(jnp.abs(z[r * ATT_BLOCK:(r + 1) * ATT_BLOCK].astype(F32)), axis=0, keepdims=True)
            stat_ref[r, row:row + 1, lo:lo + MXU_COLS] = m

    u_ref[...] = proj(0)
    base = POOL_WIDTH
    for c in range(DIFF_WIDTH // MXU_COLS):
        dq_ref[:, c * MXU_COLS:(c + 1) * MXU_COLS] = rope(proj(base + c * MXU_COLS)).astype(BF16)
    base += DIFF_WIDTH
    for ref, row, rotary in ((dk_ref, 0, True), (dv_ref, 1, False)):
        for c in range(DIFF_WIDTH // MXU_COLS):
            z = proj(base + c * MXU_COLS)
            z = (rope(z) if rotary else z).astype(BF16)
            ref[:, c * MXU_COLS:(c + 1) * MXU_COLS] = z
            put_absmax(dstat_ref, row, z, c * MXU_COLS)
        base += DIFF_WIDTH
    mq_ref[...] = rope(proj(base)).astype(BF16)
    base += MOBA_WIDTH
    k = rope(proj(base))
    mk_ref[...] = k.astype(BF16)
    put_absmax(mstat_ref, 0, k.astype(BF16), 0)
    for r in range(ROW_TILE // MOBA_BLOCK):
        kmean_ref[r] = jnp.mean(k[r * MOBA_BLOCK:(r + 1) * MOBA_BLOCK], axis=0, keepdims=True)
    base += MOBA_WIDTH
    v = proj(base).astype(BF16)
    mv_ref[...] = v
    put_absmax(mstat_ref, 1, v, 0)


def _inproj(x, g, w, cos, sin, layer, seq, cast_weights):
    t, d = x.shape
    nt = t // ROW_TILE
    tiles_per_seq = seq // ROW_TILE
    blocks_per_tile = ROW_TILE // MOBA_BLOCK
    casts = _SideCasts(cast_weights, layer, nt, lambda i: i)

    def rows(width):
        return pl.BlockSpec((ROW_TILE, width), lambda i: (i, 0))

    table = pl.BlockSpec((ROW_TILE, LANES), lambda i: (i % tiles_per_seq, 0))
    widths = (POOL_WIDTH, DIFF_WIDTH, DIFF_WIDTH, DIFF_WIDTH, MOBA_WIDTH, MOBA_WIDTH, MOBA_WIDTH)
    dtypes = (F32, BF16, BF16, BF16, BF16, BF16, BF16)
    out_shape = [jax.ShapeDtypeStruct((t, wd), dt) for wd, dt in zip(widths, dtypes)]
    out_shape.append(jax.ShapeDtypeStruct((t // MOBA_BLOCK, 1, MOBA_WIDTH), F32))
    out_specs = [rows(wd) for wd in widths]
    out_specs.append(pl.BlockSpec((blocks_per_tile, 1, MOBA_WIDTH), lambda i: (i, 0, 0)))
    for wd in (DIFF_WIDTH, MOBA_WIDTH):
        out_shape.append(jax.ShapeDtypeStruct((t // ATT_BLOCK, 2, wd), F32))
        out_specs.append(pl.BlockSpec((blocks_per_tile, 2, wd), lambda i: (i, 0, 0)))
    assert len(out_shape) == _INPROJ_OUTPUTS
    outs = pl.pallas_call(
        _inproj_body,
        grid=(nt,),
        in_specs=[rows(d), _layer(g, layer), _layer(w, layer), table, table] + casts.in_specs,
        out_specs=out_specs + casts.out_specs,
        out_shape=out_shape + casts.out_shape,
        compiler_params=pltpu.CompilerParams(dimension_semantics=("parallel",), vmem_limit_bytes=VMEM_LIMIT),
        name="inproj",
    )(x, g, w, cos, sin, *cast_weights)
    return list(outs)


def _pool_body(u_ref, w_ref, scale_ref, o_ref):
    s = u_ref.shape[0]
    row = lax.broadcasted_iota(jnp.int32, (s, 1), 0)
    groups_per_tile = LANES // POOL_GROUP_DIM
    lane_group = lax.broadcasted_iota(jnp.int32, (1, LANES), 1) // POOL_GROUP_DIM
    tiles = []
    for t in range(POOL_WIDTH // LANES):
        u = u_ref[:, t * LANES:(t + 1) * LANES]
        window_sum, shift, pooled = u, 1, None
        for gi in range(groups_per_tile):
            wdw = POOL_WINDOWS[t * groups_per_tile + gi]
            while shift < wdw:
                shifted = jnp.where(row >= shift, pltpu.roll(window_sum, shift, axis=0), 0.0)
                window_sum = window_sum + shifted
                shift *= 2
            mean = window_sum * (1.0 / jnp.minimum(row + 1, wdw).astype(F32))
            pooled = mean if pooled is None else jnp.where(lane_group == gi, mean, pooled)
        tiles.append(pooled - u)
    mixed = _dot(jnp.concatenate(tiles, axis=1), w_ref[...])
    o_ref[...] = (mixed * scale_ref[...]).astype(BF16)


def _pool(u, w_blockdiag, scale, layer, seq):
    t = u.shape[0]
    blk = pl.BlockSpec((seq, POOL_WIDTH), lambda b: (b, 0))
    return pl.pallas_call(
        _pool_body,
        grid=(t // seq,),
        in_specs=[blk, _layer(w_blockdiag, layer), _layer(scale, layer)],
        out_specs=blk,
        out_shape=jax.ShapeDtypeStruct((t, POOL_WIDTH), BF16),
        compiler_params=pltpu.CompilerParams(dimension_semantics=("parallel",), vmem_limit_bytes=VMEM_LIMIT),
        name="pool",
    )(u, w_blockdiag, scale)


def _channel_is_first(n):
    ch = lax.broadcasted_iota(jnp.int32, (n, 1), 0)
    return (ch // QUARTER) % 2 == 0


def _blk(j):
    return slice(j * ATT_BLOCK, (j + 1) * ATT_BLOCK)


def _transpose_values(v_ref, vt_scr):
    for j in range(v_ref.shape[0] // ATT_BLOCK):
        vt_scr[:, _blk(j)] = v_ref[_blk(j), :].astype(F32).T.astype(BF16)


def _causal_mask():
    key = lax.broadcasted_iota(jnp.int32, (ATT_BLOCK, ATT_BLOCK), 0)
    qry = lax.broadcasted_iota(jnp.int32, (ATT_BLOCK, ATT_BLOCK), 1)
    return key <= qry


def _score_bounds(q_ref, stat_ref):
    nq = stat_ref.shape[1] // 2
    rows, run = [], None
    for j in range(nq):
        m = stat_ref[0, 2 * j:2 * j + 1, :]
        run = m if run is None else jnp.maximum(run, m)
        rows.append(run)
    kabs = jnp.concatenate(rows, axis=0)
    lane_first = (lax.broadcasted_iota(jnp.int32, (1, LANES), 1) // QUARTER) % 2 == 0
    ka = jnp.concatenate([jnp.where(lane_first, kabs, 0.0), jnp.where(lane_first, 0.0, kabs)], axis=0)
    b = lax.dot_general(ka.astype(BF16), jnp.abs(q_ref[...]), (((1,), (1,)), ((), ())), preferred_element_type=F32)
    return b * (1.0 + 2.0 ** -10) + 2.0 ** -10


class _Stream:
    def __init__(self, i, qt, allow, bound, vt, slot, tag):
        self.i, self.qt, self.allow, self.bound, self.vt, self.slot, self.tag = i, qt, allow, bound, vt, slot, tag
        self.mx = self.l = self.acc = None
        self.prestored = False


def _masked_scores(st, j, k_ref, s_scr, causal):
    if st.prestored:
        return s_scr[st.slot, _blk(j), :]
    s = _dot(k_ref[_blk(j), :], st.qt)
    if j == st.i:
        s = jnp.where(causal, s, NEG_INF)
    elif st.allow is not None:
        s = jnp.where(st.allow[j:j + 1, :] > 0.5, s, NEG_INF)
    return s


def _prestore_scores(st, k_ref, s_scr):
    causal = _causal_mask()
    for j in range(st.i + 1):
        s_scr[st.slot, _blk(j), :] = _masked_scores(st, j, k_ref, s_scr, causal)
    st.prestored = True


def _score_phase(st, k_ref, s_scr, causal):
    mx = None
    for j in range(st.i + 1):
        s = _masked_scores(st, j, k_ref, s_scr, causal)
        if not st.prestored:
            s_scr[st.slot, _blk(j), :] = s
        bm = jnp.max(s, axis=0, keepdims=True)
        mx = bm if mx is None else jnp.maximum(mx, bm)
        yield
    st.mx = mx


def _exp_phase(st, s_scr, p_scr):
    for j in range(st.i + 1):
        _emit_probabilities(st, j, jnp.exp2(s_scr[st.slot, _blk(j), :] - st.mx), p_scr)
        yield


def _bounded_phase(st, k_ref, s_scr, p_scr, causal):
    for j in range(st.i + 1):
        _emit_probabilities(st, j, jnp.exp2(_masked_scores(st, j, k_ref, s_scr, causal) - st.bound), p_scr)
        yield


def _emit_probabilities(st, j, p, p_scr):
    p_scr[st.slot, _blk(j), :] = p.astype(BF16)
    bl = jnp.sum(p, axis=0, keepdims=True)
    st.l = bl if j == 0 else st.l + bl


def _value_phase(st, p_scr):
    n = (st.i + 1) * ATT_BLOCK
    st.acc = _dot(st.vt[:, 0:n], p_scr[st.slot, 0:n, :])
    yield


def _run_streams(streams, k_ref, s_scr, p_scr, causal, on_done, bounded):
    def stages(st):
        if bounded:
            return [_bounded_phase(st, k_ref, s_scr, p_scr, causal), _value_phase(st, p_scr)]
        return [_score_phase(st, k_ref, s_scr, causal), _exp_phase(st, s_scr, p_scr), _value_phase(st, p_scr)]

    depth = 2 if bounded else 3
    staged = [stages(st) for st in streams]
    for tick in range(len(streams) + depth - 1):
        live = [staged[tick - d][d] for d in range(depth) if 0 <= tick - d < len(streams)]
        while live:
            for g in list(live):
                if next(g, _DONE) is _DONE:
                    live.remove(g)
        if tick >= depth - 1:
            on_done(streams[tick - depth + 1])


_DONE = object()


def _attention(make_streams, bounds, k_ref, stat_ref, s_scr, p_scr, on_done):
    nq = bounds.shape[0] // 2
    worst = jnp.maximum(bounds[nq - 1:nq, :], bounds[2 * nq - 1:2 * nq, :])
    v_abs = jnp.max(functools.reduce(jnp.maximum, [stat_ref[0, 2 * j + 1:2 * j + 2, :] for j in range(nq)]))
    safe = jnp.logical_and(jnp.max(worst) <= SAFE_BOUND, v_abs <= VALUE_LIMIT)

    @pl.when(safe)
    def _():
        _run_streams(make_streams(), k_ref, s_scr, p_scr, _causal_mask(), on_done, bounded=True)

    @pl.when(jnp.logical_not(safe))
    def _():
        _run_streams(make_streams(), k_ref, s_scr, p_scr, _causal_mask(), on_done, bounded=False)


def _diff_body(q_ref, k_ref, v_ref, stat_ref, lam_ref, g_ref, o_ref, vt_scr, s_scr, p_scr, *, lambda_init):
    lam = lam_ref[...]
    lam_val = (jnp.exp(jnp.sum(lam[0:1] * lam[1:2], keepdims=True))
               - jnp.exp(jnp.sum(lam[2:3] * lam[3:4], keepdims=True)) + lambda_init)
    gain = g_ref[...] * (1.0 - lambda_init)
    nq = q_ref.shape[0] // ATT_BLOCK
    bounds = _score_bounds(q_ref, stat_ref)
    first = _channel_is_first(LANES)

    def build(i, mi, qts):
        if i not in qts:
            qts[i] = q_ref[_blk(i), :].astype(F32).T
        qm = jnp.where(first, qts[i], 0.0) if mi == 0 else jnp.where(first, 0.0, qts[i])
        bound = bounds[mi * nq + i:mi * nq + i + 1, _blk(i)] - SCORE_SHIFT
        return _Stream(i, qm.astype(BF16), None, bound, vt_scr, 2 * (i % 2) + mi, mi)

    head_qts = {}
    head = build(nq - 1, 0, head_qts)
    _prestore_scores(head, k_ref, s_scr)

    def make_streams():
        _transpose_values(v_ref, vt_scr)
        qts = dict(head_qts)
        order = [(i, mi) for i in reversed(range(nq)) for mi in range(2)]
        return [head] + [build(i, mi, qts) for i, mi in order[1:]]

    outs = {}

    def on_done(st):
        outs[st.tag] = st.acc * (1.0 / st.l)
        if st.tag == 1:
            ot = outs[0] - lam_val * outs[1]
            o_ref[_blk(st.i), :] = _rms(ot.T, gain).astype(BF16)

    _attention(make_streams, bounds, k_ref, stat_ref, s_scr, p_scr, on_done)


def _stat_spec(stat):
    return pl.BlockSpec((1, stat.shape[1], LANES), lambda b, h: (b, 0, h))


def _diff_attention(dq, dk, dv, stat, lam, subln, layer, lambda_init, seq):
    t = dq.shape[0]
    blk = pl.BlockSpec((seq, LANES), lambda b, h: (b, h))
    return pl.pallas_call(
        functools.partial(_diff_body, lambda_init=lambda_init),
        grid=(t // seq, DIFF_HEADS),
        in_specs=[blk, blk, blk, _stat_spec(stat), _layer(lam, layer), _layer(subln, layer)],
        out_specs=blk,
        out_shape=jax.ShapeDtypeStruct((t, DIFF_WIDTH), BF16),
        scratch_shapes=[pltpu.VMEM((LANES, seq), BF16), pltpu.VMEM((4, seq, ATT_BLOCK), F32),
                        pltpu.VMEM((4, seq, ATT_BLOCK), BF16)],
        compiler_params=pltpu.CompilerParams(
            dimension_semantics=("parallel", "parallel"), vmem_limit_bytes=VMEM_LIMIT),
        name="diff_attn",
    )(dq, dk, dv, stat, lam, subln)


_MOBA_INPUTS = 5
_MOBA_SCRATCH = 3


def _moba_allow(gate, own, n_sel):
    nb = gate.shape[0]
    blk = lax.broadcasted_iota(jnp.int32, (nb, 1), 0)
    rank = jnp.zeros(gate.shape, jnp.int32)
    for other in range(own):
        g_other = gate[other:other + 1, :]
        beats = (g_other > gate) | ((g_other == gate) & (other < blk))
        rank = rank + jnp.where(beats, 1, 0)
    return jnp.where((blk < own) & (rank < n_sel), 1.0, 0.0)


def _moba_body(*refs):
    n_cast = (len(refs) - _MOBA_INPUTS - 1 - _MOBA_SCRATCH) // 2
    q_ref, k_ref, v_ref, stat_ref, kmean_ref = refs[:_MOBA_INPUTS]
    cast_in = refs[_MOBA_INPUTS:_MOBA_INPUTS + n_cast]
    o_ref = refs[_MOBA_INPUTS + n_cast]
    cast_out = refs[_MOBA_INPUTS + n_cast + 1:_MOBA_INPUTS + 2 * n_cast + 1]
    vt_scr, s_scr, p_scr = refs[-_MOBA_SCRATCH:]
    _cast_slabs(cast_in, cast_out)

    nq = q_ref.shape[0] // ATT_BLOCK
    n_sel = min(MOBA_TOPK, nq - 1)
    bounds = _score_bounds(q_ref, stat_ref)
    first = _channel_is_first(LANES)
    kmean = kmean_ref[0]

    def build(i, hd, qts):
        if i not in qts:
            qts[i] = q_ref[_blk(i), :].astype(F32).T
        qh = jnp.where(first, qts[i], 0.0) if hd == 0 else jnp.where(first, 0.0, qts[i])
        allow = None
        if i > n_sel:
            gate = jnp.dot(kmean, qh, preferred_element_type=F32, precision=lax.Precision.HIGHEST)
            allow = _moba_allow(gate, i, n_sel)
        bound = bounds[hd * nq + i:hd * nq + i + 1, _blk(i)] - SCORE_SHIFT
        vt = vt_scr.at[hd * HEAD_DIM:(hd + 1) * HEAD_DIM, :]
        return _Stream(i, qh.astype(BF16), allow, bound, vt, 2 * (i % 2) + hd, hd)

    head_qts = {}
    head = build(nq - 1, 0, head_qts)
    _prestore_scores(head, k_ref, s_scr)

    def make_streams():
        _transpose_values(v_ref, vt_scr)
        qts = dict(head_qts)
        order = [(i, hd) for i in reversed(range(nq)) for hd in range(2)]
        return [head] + [build(i, hd, qts) for i, hd in order[1:]]

    outs = {}

    def on_done(st):
        outs[st.tag] = st.acc * (1.0 / st.l)
        if st.tag == 1:
            o_ref[_blk(st.i), :] = jnp.concatenate([outs[0], outs[1]], axis=0).T.astype(BF16)

    _attention(make_streams, bounds, k_ref, stat_ref, s_scr, p_scr, on_done)


def _moba_attention(mq, mk, mv, stat, kmean, seq, cast_weights, cast_layer):
    t = mq.shape[0]
    nq = seq // ATT_BLOCK
    pairs = MOBA_WIDTH // LANES
    casts = _SideCasts(cast_weights, cast_layer, (t // seq) * pairs, lambda b, p: b * pairs + p)
    blk = pl.BlockSpec((seq, LANES), lambda b, p: (b, p))
    km = pl.BlockSpec((1, nq, LANES), lambda b, p: (b, 0, p))
    outs = pl.pallas_call(
        _moba_body,
        grid=(t // seq, pairs),
        in_specs=[blk, blk, blk, _stat_spec(stat), km] + casts.in_specs,
        out_specs=[blk] + casts.out_specs,
        out_shape=[jax.ShapeDtypeStruct((t, MOBA_WIDTH), BF16)] + casts.out_shape,
        scratch_shapes=[pltpu.VMEM((LANES, seq), BF16), pltpu.VMEM((4, seq, ATT_BLOCK), F32),
                        pltpu.VMEM((4, seq, ATT_BLOCK), BF16)],
        compiler_params=pltpu.CompilerParams(
            dimension_semantics=("parallel", "parallel"), vmem_limit_bytes=VMEM_LIMIT),
        name="moba_attn",
    )(mq, mk, mv, stat, kmean, *cast_weights)
    return list(outs)


def _inproj_weights(w):
    depth, d, _ = w.shape
    q_scale = HEAD_DIM ** -0.5 * math.log2(math.e)

    def pair_interleave(sec):
        width = sec.shape[-1]
        return sec.reshape(depth, d, width // LANES, 2, 2, QUARTER).swapaxes(3, 4).reshape(depth, d, width)

    parts, base = [], 0
    for width, rotary, scale in ((POOL_WIDTH, False, None),
                                 (DIFF_WIDTH, True, q_scale), (DIFF_WIDTH, True, None), (DIFF_WIDTH, False, None),
                                 (MOBA_WIDTH, True, q_scale), (MOBA_WIDTH, True, None), (MOBA_WIDTH, False, None)):
        sec = w[:, :, base:base + width]
        if rotary:
            sec = pair_interleave(sec)
        if scale is not None:
            sec = sec * scale
        parts.append(sec)
        base += width
    return jnp.concatenate(parts, axis=-1).astype(BF16)


def _rope_tables(seq):
    inv = ROPE_THETA ** (-jnp.arange(0, HEAD_DIM, 2, dtype=F32) / HEAD_DIM)
    ang = jnp.arange(seq, dtype=F32)[:, None] * inv[None, :]
    cos, sin = jnp.cos(ang), jnp.sin(ang)
    return jnp.tile(cos, (1, 4)), jnp.concatenate([-sin, -sin, sin, sin], axis=1)


def kernel(x, ffn1_norm, ffn1_w_in, ffn1_w_out, mix_norm, mix_w_in, mix_w_out, pool_w, pool_scale,
           diff_lambda, diff_subln, ffn2_norm, ffn2_w_in, ffn2_w_out, final_norm):
    batch, seq, d = x.shape
    depth = ffn1_norm.shape[0]
    t = batch * seq
    assert seq % ROW_TILE == 0 and ROW_TILE % MOBA_BLOCK == 0 and ffn1_w_out.shape[1] % FF_CHUNK == 0

    cos, sin = _rope_tables(seq)
    gf = final_norm.reshape(1, d)
    g1, gm, g2 = (g.reshape(depth, 1, d) for g in (ffn1_norm, mix_norm, ffn2_norm))
    w_mix_in = _inproj_weights(mix_w_in)
    groups = len(POOL_WINDOWS)
    pool_bd = jnp.einsum("lgcd,gh->lgchd", pool_w, jnp.eye(groups, dtype=F32)).reshape(depth, POOL_WIDTH, POOL_WIDTH)
    pool_sc = pool_scale.reshape(depth, 1, POOL_WIDTH)
    subln = diff_subln.reshape(depth, 1, LANES)
    nblk = seq // ATT_BLOCK

    w1_in, w1_out = _cast_layer([ffn1_w_in, ffn1_w_out], 0)
    xt = x.reshape(t, d)
    for l in range(depth):
        lambda_init = 0.8 - 0.6 * math.exp(-0.3 * l)
        xt = _ffn(xt, g1, w1_in, w1_out, gf, l, 0, False)
        (u, dq, dk, dv, mq, mk, mv, kmean, dstat, mstat, w2_in, w2_out, w_mix_out) = _inproj(
            xt, gm, w_mix_in, cos, sin, l, seq, [ffn2_w_in, ffn2_w_out, mix_w_out])
        dstat = dstat.reshape(batch, 2 * nblk, DIFF_WIDTH)
        mstat = mstat.reshape(batch, 2 * nblk, MOBA_WIDTH)
        ya = _pool(u, pool_bd, pool_sc, l, seq)
        yb = _diff_attention(dq, dk, dv, dstat, diff_lambda, subln, l, lambda_init, seq)
        next_ffn1 = [ffn1_w_in, ffn1_w_out] if l + 1 < depth else []
        yc, *next_w1 = _moba_attention(mq, mk, mv, mstat, kmean.reshape(batch, nblk, MOBA_WIDTH), seq,
                                       next_ffn1, l + 1)
        xt = _ffn(xt, g2, w2_in, w2_out, gf, l, 0, l == depth - 1, mix=(ya, yb, yc, w_mix_out))
        if next_w1:
            w1_in, w1_out = next_w1
    return xt.reshape(batch, seq, d)
```

```python
import functools
import math

import jax
import jax.numpy as jnp
from jax import lax
from jax.experimental import pallas as pl
from jax.experimental.pallas import tpu as pltpu

HEAD_DIM = 64
POOL_WINDOWS = (2, 4, 8, 16)
POOL_GROUP_DIM = 64
POOL_WIDTH = 256
DIFF_WIDTH = 512
DIFF_HEADS = 4
MOBA_WIDTH = 256
MOBA_BLOCK = 256
MOBA_TOPK = 3
ROPE_THETA = 10000.0
NORM_EPS = 1e-6
NEG_INF = -1e30

LANES = 128
BF16_ROWS = 16
MXU_COLS = 256
assert POOL_WIDTH == MXU_COLS and MOBA_WIDTH == MXU_COLS and DIFF_WIDTH % MXU_COLS == 0
QUARTER = HEAD_DIM // 2
ATT_BLOCK = 256
ROW_TILE = 1024
FF_CHUNK = 256
CAST_STEPS = 16
SCORE_SHIFT = 40.0
SAFE_BOUND = 70.0
VALUE_LIMIT = 2.0 ** 60
VMEM_LIMIT = 56 * 1024 * 1024

F32 = jnp.float32
BF16 = jnp.bfloat16


def _dot(a, b):
    return jnp.dot(a, b, preferred_element_type=F32)


def _rms(x, g):
    return x * lax.rsqrt(jnp.mean(x * x, axis=-1, keepdims=True) + NORM_EPS) * g


def _resident(shape):
    nd = len(shape)
    return pl.BlockSpec(shape, lambda *_: (0,) * nd, pipeline_mode=pl.Buffered(1))


def _layer(arr, layer):
    tail = (0,) * (arr.ndim - 1)
    return pl.BlockSpec((None,) + arr.shape[1:], lambda *_: (layer,) + tail, pipeline_mode=pl.Buffered(1))


class _SideCasts:
    def __init__(self, weights, layer, steps, step_of):
        self.in_specs, self.out_shape, self.out_specs = [], [], []
        for w in weights:
            _, a, b = w.shape
            col_parts = next(c for c in range(1, steps + 1) if steps % c == 0 and b % (c * LANES) == 0
                             and a % ((steps // c) * BF16_ROWS) == 0)
            slab = (a // (steps // col_parts), b // col_parts)

            def index(*g, _layer=layer, _cp=col_parts):
                step = step_of(*g)
                return (_layer, step // _cp, step % _cp)

            self.in_specs.append(pl.BlockSpec((None,) + slab, index))
            self.out_shape.append(jax.ShapeDtypeStruct((1, a, b), BF16))
            self.out_specs.append(pl.BlockSpec((None,) + slab, functools.partial(index, _layer=0)))


def _cast_slabs(in_refs, out_refs):
    for src, dst in zip(in_refs, out_refs):
        dst[...] = src[...].astype(BF16)


def _cast_body(*refs):
    _cast_slabs(refs[:len(refs) // 2], refs[len(refs) // 2:])


def _cast_layer(weights, layer):
    casts = _SideCasts(weights, layer, CAST_STEPS, lambda i: i)
    return pl.pallas_call(
        _cast_body,
        grid=(CAST_STEPS,),
        in_specs=casts.in_specs,
        out_specs=casts.out_specs,
        out_shape=casts.out_shape,
        compiler_params=pltpu.CompilerParams(dimension_semantics=("parallel",), vmem_limit_bytes=VMEM_LIMIT),
        name="cast",
    )(*weights)


def _ffn_body(x_ref, g_ref, win_ref, wout_ref, gf_ref, *rest, d_ff, final_norm, with_mix):
    if with_mix:
        ya_ref, yb_ref, yc_ref, wmix_ref, o_ref, u_scr = rest
        lo_b, lo_c = POOL_WIDTH, POOL_WIDTH + DIFF_WIDTH
        x = (x_ref[...] + _dot(ya_ref[...], wmix_ref[0:lo_b, :]) + _dot(yb_ref[...], wmix_ref[lo_b:lo_c, :])
             + _dot(yc_ref[...], wmix_ref[lo_c:, :]))
    else:
        o_ref, u_scr = rest
        x = x_ref[...]
    h = _rms(x, g_ref[...]).astype(BF16)
    for c in range(d_ff // FF_CHUNK):
        lo = c * FF_CHUNK
        a = _dot(h, win_ref[:, lo:lo + FF_CHUNK])
        b = _dot(h, win_ref[:, d_ff + lo:d_ff + lo + FF_CHUNK])
        u_scr[:, lo:lo + FF_CHUNK] = (a / (1.0 + jnp.exp(-a)) * b).astype(BF16)
    y = x + 0.5 * _dot(u_scr[...], wout_ref[...])
    if final_norm:
        y = _rms(y, gf_ref[...])
    o_ref[...] = y


def _ffn(x, g, w_in, w_out, gf, g_layer, w_layer, final_norm, mix=None):
    t, d = x.shape
    d_ff = w_out.shape[1]

    def rows(width):
        return pl.BlockSpec((ROW_TILE, width), lambda i: (i, 0))

    args = [x, g, w_in, w_out, gf]
    in_specs = [rows(d), _layer(g, g_layer), _layer(w_in, w_layer), _layer(w_out, w_layer), _resident((1, d))]
    if mix is not None:
        args.extend(mix)
        in_specs.extend([rows(POOL_WIDTH), rows(DIFF_WIDTH), rows(MOBA_WIDTH), _layer(mix[3], w_layer)])
    return pl.pallas_call(
        functools.partial(_ffn_body, d_ff=d_ff, final_norm=final_norm, with_mix=mix is not None),
        grid=(t // ROW_TILE,),
        in_specs=in_specs,
        out_specs=rows(d),
        out_shape=jax.ShapeDtypeStruct((t, d), F32),
        scratch_shapes=[pltpu.VMEM((ROW_TILE, d_ff), BF16)],
        compiler_params=pltpu.CompilerParams(dimension_semantics=("parallel",), vmem_limit_bytes=VMEM_LIMIT),
        name="mix_ffn" if mix is not None else "ffn",
    )(*args)


_INPROJ_INPUTS = 5
_INPROJ_OUTPUTS = 10


def _inproj_body(*refs):
    n_cast = (len(refs) - _INPROJ_INPUTS - _INPROJ_OUTPUTS) // 2
    x_ref, g_ref, w_ref, cos_ref, sin_ref = refs[:_INPROJ_INPUTS]
    cast_in = refs[_INPROJ_INPUTS:_INPROJ_INPUTS + n_cast]
    outs = refs[_INPROJ_INPUTS + n_cast:]
    u_ref, dq_ref, dk_ref, dv_ref, mq_ref, mk_ref, mv_ref, kmean_ref, dstat_ref, mstat_ref = outs[:_INPROJ_OUTPUTS]
    _cast_slabs(cast_in, outs[_INPROJ_OUTPUTS:])

    h = _rms(x_ref[...], g_ref[...]).astype(BF16)
    cos = cos_ref[...]
    sin = sin_ref[...]

    def proj(lo):
        return _dot(h, w_ref[:, lo:lo + MXU_COLS])

    def rope(z):
        halves = [z[:, c * LANES:(c + 1) * LANES] for c in range(MXU_COLS // LANES)]
        return jnp.concatenate([zc * cos + pltpu.roll(zc, 2 * QUARTER, axis=1) * sin for zc in halves], axis=1)

    def put_absmax(stat_ref, row, z, lo):
        for r in range(ROW_TILE // ATT_BLOCK):
            m = jnp.max(jnp.abs(z[r * ATT_BLOCK:(r + 1) * ATT_BLOCK].astype(F32)), axis=0, keepdims=True)
            stat_ref[r, row:row + 1, lo:lo + MXU_COLS] = m

    u_ref[...] = proj(0)
    base = POOL_WIDTH
    for c in range(DIFF_WIDTH // MXU_COLS):
        dq_ref[:, c * MXU_COLS:(c + 1) * MXU_COLS] = rope(proj(base + c * MXU_COLS)).astype(BF16)
    base += DIFF_WIDTH
    for ref, row, rotary in ((dk_ref, 0, True), (dv_ref, 1, False)):
        for c in range(DIFF_WIDTH // MXU_COLS):
            z = proj(base + c * MXU_COLS)
            z = (rope(z) if rotary else z).astype(BF16)
            ref[:, c * MXU_COLS:(c + 1) * MXU_COLS] = z
            put_absmax(dstat_ref, row, z, c * MXU_COLS)
        base += DIFF_WIDTH
    mq_ref[...] = rope(proj(base)).astype(BF16)
    base += MOBA_WIDTH
    k = rope(proj(base))
    mk_ref[...] = k.astype(BF16)
    put_absmax(mstat_ref, 0, k.astype(BF16), 0)
    for r in range(ROW_TILE // MOBA_BLOCK):
        kmean_ref[r] = jnp.mean(k[r * MOBA_BLOCK:(r + 1) * MOBA_BLOCK], axis=0, keepdims=True)
    base += MOBA_WIDTH
    v = proj(base).astype(BF16)
    mv_ref[...] = v
    put_absmax(mstat_ref, 1, v, 0)


def _inproj(x, g, w, cos, sin, layer, seq, cast_weights):
    t, d = x.shape
    nt = t // ROW_TILE
    tiles_per_seq = seq // ROW_TILE
    blocks_per_tile = ROW_TILE // MOBA_BLOCK
    casts = _SideCasts(cast_weights, layer, nt, lambda i: i)

    def rows(width):
        return pl.BlockSpec((ROW_TILE, width), lambda i: (i, 0))

    table = pl.BlockSpec((ROW_TILE, LANES), lambda i: (i % tiles_per_seq, 0))
    widths = (POOL_WIDTH, DIFF_WIDTH, DIFF_WIDTH, DIFF_WIDTH, MOBA_WIDTH, MOBA_WIDTH, MOBA_WIDTH)
    dtypes = (F32, BF16, BF16, BF16, BF16, BF16, BF16)
    out_shape = [jax.ShapeDtypeStruct((t, wd), dt) for wd, dt in zip(widths, dtypes)]
    out_shape.append(jax.ShapeDtypeStruct((t // MOBA_BLOCK, 1, MOBA_WIDTH), F32))
    out_specs = [rows(wd) for wd in widths]
    out_specs.append(pl.BlockSpec((blocks_per_tile, 1, MOBA_WIDTH), lambda i: (i, 0, 0)))
    for wd in (DIFF_WIDTH, MOBA_WIDTH):
        out_shape.append(jax.ShapeDtypeStruct((t // ATT_BLOCK, 2, wd), F32))
        out_specs.append(pl.BlockSpec((blocks_per_tile, 2, wd), lambda i: (i, 0, 0)))
    assert len(out_shape) == _INPROJ_OUTPUTS
    outs = pl.pallas_call(
        _inproj_body,
        grid=(nt,),
        in_specs=[rows(d), _layer(g, layer), _layer(w, layer), table, table] + casts.in_specs,
        out_specs=out_specs + casts.out_specs,
        out_shape=out_shape + casts.out_shape,
        compiler_params=pltpu.CompilerParams(dimension_semantics=("parallel",), vmem_limit_bytes=VMEM_LIMIT),
        name="inproj",
    )(x, g, w, cos, sin, *cast_weights)
    return list(outs)


def _pool_body(u_ref, w_ref, scale_ref, o_ref):
    s = u_ref.shape[0]
    row = lax.broadcasted_iota(jnp.int32, (s, 1), 0)
    groups_per_tile = LANES // POOL_GROUP_DIM
    lane_group = lax.broadcasted_iota(jnp.int32, (1, LANES), 1) // POOL_GROUP_DIM
    tiles = []
    for t in range(POOL_WIDTH // LANES):
        u = u_ref[:, t * LANES:(t + 1) * LANES]
        window_sum, shift, pooled = u, 1, None
        for gi in range(groups_per_tile):
            wdw = POOL_WINDOWS[t * groups_per_tile + gi]
            while shift < wdw:
                shifted = jnp.where(row >= shift, pltpu.roll(window_sum, shift, axis=0), 0.0)
                window_sum = window_sum + shifted
                shift *= 2
            mean = window_sum * (1.0 / jnp.minimum(row + 1, wdw).astype(F32))
            pooled = mean if pooled is None else jnp.where(lane_group == gi, mean, pooled)
        tiles.append(pooled - u)
    mixed = _dot(jnp.concatenate(tiles, axis=1), w_ref[...])
    o_ref[...] = (mixed * scale_ref[...]).astype(BF16)


def _pool(u, w_blockdiag, scale, layer, seq):
    t = u.shape[0]
    blk = pl.BlockSpec((seq, POOL_WIDTH), lambda b: (b, 0))
    return pl.pallas_call(
        _pool_body,
        grid=(t // seq,),
        in_specs=[blk, _layer(w_blockdiag, layer), _layer(scale, layer)],
        out_specs=blk,
        out_shape=jax.ShapeDtypeStruct((t, POOL_WIDTH), BF16),
        compiler_params=pltpu.CompilerParams(dimension_semantics=("parallel",), vmem_limit_bytes=VMEM_LIMIT),
        name="pool",
    )(u, w_blockdiag, scale)


def _channel_is_first(n):
    ch = lax.broadcasted_iota(jnp.int32, (n, 1), 0)
    return (ch // QUARTER) % 2 == 0


def _blk(j):
    return slice(j * ATT_BLOCK, (j + 1) * ATT_BLOCK)


def _transpose_values(v_ref, vt_scr):
    for j in range(v_ref.shape[0] // ATT_BLOCK):
        vt_scr[:, _blk(j)] = v_ref[_blk(j), :].astype(F32).T.astype(BF16)


def _causal_mask():
    key = lax.broadcasted_iota(jnp.int32, (ATT_BLOCK, ATT_BLOCK), 0)
    qry = lax.broadcasted_iota(jnp.int32, (ATT_BLOCK, ATT_BLOCK), 1)
    return key <= qry


def _score_bounds(q_ref, stat_ref):
    nq = stat_ref.shape[0]
    rows, run = [], None
    for j in range(nq):
        m = stat_ref[j, 0:1, :]
        run = m if run is None else jnp.maximum(run, m)
        rows.append(run)
    kabs = jnp.concatenate(rows, axis=0)
    lane_first = (lax.broadcasted_iota(jnp.int32, (1, LANES), 1) // QUARTER) % 2 == 0
    ka = jnp.concatenate([jnp.where(lane_first, kabs, 0.0), jnp.where(lane_first, 0.0, kabs)], axis=0)
    b = lax.dot_general(ka.astype(BF16), jnp.abs(q_ref[...]), (((1,), (1,)), ((), ())), preferred_element_type=F32)
    return b * (1.0 + 2.0 ** -10) + 2.0 ** -10


class _Stream:
    def __init__(self, i, qt, allow, bound, vt, slot, tag):
        self.i, self.qt, self.allow, self.bound, self.vt, self.slot, self.tag = i, qt, allow, bound, vt, slot, tag
        self.mx = self.l = self.acc = None
        self.prestored = False


def _masked_scores(st, j, k_ref, s_scr, causal):
    if st.prestored:
        return s_scr[st.slot, _blk(j), :]
    s = _dot(k_ref[_blk(j), :], st.qt)
    if j == st.i:
        s = jnp.where(causal, s, NEG_INF)
    elif st.allow is not None:
        s = jnp.where(st.allow[j:j + 1, :] > 0.5, s, NEG_INF)
    return s


def _prestore_scores(st, k_ref, s_scr):
    causal = _causal_mask()
    for j in range(st.i + 1):
        s_scr[st.slot, _blk(j), :] = _masked_scores(st, j, k_ref, s_scr, causal)
    st.prestored = True


def _score_phase(st, k_ref, s_scr, causal):
    mx = None
    for j in range(st.i + 1):
        s = _masked_scores(st, j, k_ref, s_scr, causal)
        if not st.prestored:
            s_scr[st.slot, _blk(j), :] = s
        bm = jnp.max(s, axis=0, keepdims=True)
        mx = bm if mx is None else jnp.maximum(mx, bm)
        yield
    st.mx = mx


def _exp_phase(st, s_scr, p_scr):
    for j in range(st.i + 1):
        _emit_probabilities(st, j, jnp.exp2(s_scr[st.slot, _blk(j), :] - st.mx), p_scr)
        yield


def _bounded_phase(st, k_ref, s_scr, p_scr, causal):
    for j in range(st.i + 1):
        _emit_probabilities(st, j, jnp.exp2(_masked_scores(st, j, k_ref, s_scr, causal) - st.bound), p_scr)
        yield


def _emit_probabilities(st, j, p, p_scr):
    p_scr[st.slot, _blk(j), :] = p.astype(BF16)
    bl = jnp.sum(p, axis=0, keepdims=True)
    st.l = bl if j == 0 else st.l + bl


def _value_phase(st, p_scr):
    n = (st.i + 1) * ATT_BLOCK
    st.acc = _dot(st.vt[:, 0:n], p_scr[st.slot, 0:n, :])
    yield


def _run_streams(streams, k_ref, s_scr, p_scr, causal, on_done, bounded):
    def stages(st):
        if bounded:
            return [_bounded_phase(st, k_ref, s_scr, p_scr, causal), _value_phase(st, p_scr)]
        return [_score_phase(st, k_ref, s_scr, causal), _exp_phase(st, s_scr, p_scr), _value_phase(st, p_scr)]

    depth = 2 if bounded else 3
    staged = [stages(st) for st in streams]
    for tick in range(len(streams) + depth - 1):
        live = [staged[tick - d][d] for d in range(depth) if 0 <= tick - d < len(streams)]
        while live:
            for g in list(live):
                if next(g, _DONE) is _DONE:
                    live.remove(g)
        if tick >= depth - 1:
            on_done(streams[tick - depth + 1])


_DONE = object()


def _attention(make_streams, bounds, k_ref, stat_ref, s_scr, p_scr, on_done):
    nq = bounds.shape[0] // 2
    worst = jnp.maximum(bounds[nq - 1:nq, :], bounds[2 * nq - 1:2 * nq, :])
    v_abs = jnp.max(functools.reduce(jnp.maximum, [stat_ref[j, 1:2, :] for j in range(nq)]))
    safe = jnp.logical_and(jnp.max(worst) <= SAFE_BOUND, v_abs <= VALUE_LIMIT)

    @pl.when(safe)
    def _():
        _run_streams(make_streams(), k_ref, s_scr, p_scr, _causal_mask(), on_done, bounded=True)

    @pl.when(jnp.logical_not(safe))
    def _():
        _run_streams(make_streams(), k_ref, s_scr, p_scr, _causal_mask(), on_done, bounded=False)


def _diff_body(q_ref, k_ref, v_ref, stat_ref, lam_ref, g_ref, o_ref, vt_scr, s_scr, p_scr, *, lambda_init):
    lam = lam_ref[...]
    lam_val = (jnp.exp(jnp.sum(lam[0:1] * lam[1:2], keepdims=True))
               - jnp.exp(jnp.sum(lam[2:3] * lam[3:4], keepdims=True)) + lambda_init)
    gain = g_ref[...] * (1.0 - lambda_init)
    nq = q_ref.shape[0] // ATT_BLOCK
    bounds = _score_bounds(q_ref, stat_ref)
    first = _channel_is_first(LANES)

    def build(i, mi, qts):
        if i not in qts:
            qts[i] = q_ref[_blk(i), :].astype(F32).T
        qm = jnp.where(first, qts[i], 0.0) if mi == 0 else jnp.where(first, 0.0, qts[i])
        bound = bounds[mi * nq + i:mi * nq + i + 1, _blk(i)] - SCORE_SHIFT
        return _Stream(i, qm.astype(BF16), None, bound, vt_scr, 2 * (i % 2) + mi, mi)

    head_qts = {}
    head = build(nq - 1, 0, head_qts)
    _prestore_scores(head, k_ref, s_scr)

    def make_streams():
        _transpose_values(v_ref, vt_scr)
        qts = dict(head_qts)
        order = [(i, mi) for i in reversed(range(nq)) for mi in range(2)]
        return [head] + [build(i, mi, qts) for i, mi in order[1:]]

    outs = {}

    def on_done(st):
        outs[st.tag] = st.acc * (1.0 / st.l)
        if st.tag == 1:
            ot = outs[0] - lam_val * outs[1]
            o_ref[_blk(st.i), :] = _rms(ot.T, gain).astype(BF16)

    _attention(make_streams, bounds, k_ref, stat_ref, s_scr, p_scr, on_done)


def _stat_spec(blocks):
    return pl.BlockSpec((blocks, 2, LANES), lambda b, h: (b, 0, h))


def _diff_attention(dq, dk, dv, stat, lam, subln, layer, lambda_init, seq):
    t = dq.shape[0]
    blk = pl.BlockSpec((seq, LANES), lambda b, h: (b, h))
    return pl.pallas_call(
        functools.partial(_diff_body, lambda_init=lambda_init),
        grid=(t // seq, DIFF_HEADS),
        in_specs=[blk, blk, blk, _stat_spec(seq // ATT_BLOCK), _layer(lam, layer), _layer(subln, layer)],
        out_specs=blk,
        out_shape=jax.ShapeDtypeStruct((t, DIFF_WIDTH), BF16),
        scratch_shapes=[pltpu.VMEM((LANES, seq), BF16), pltpu.VMEM((4, seq, ATT_BLOCK), F32),
                        pltpu.VMEM((4, seq, ATT_BLOCK), BF16)],
        compiler_params=pltpu.CompilerParams(
            dimension_semantics=("parallel", "parallel"), vmem_limit_bytes=VMEM_LIMIT),
        name="diff_attn",
    )(dq, dk, dv, stat, lam, subln)


_MOBA_INPUTS = 5
_MOBA_SCRATCH = 3


def _moba_allow(gate, own, n_sel):
    nb = gate.shape[0]
    blk = lax.broadcasted_iota(jnp.int32, (nb, 1), 0)
    rank = jnp.zeros(gate.shape, jnp.int32)
    for other in range(own):
        g_other = gate[other:other + 1, :]
        beats = (g_other > gate) | ((g_other == gate) & (other < blk))
        rank = rank + jnp.where(beats, 1, 0)
    return jnp.where((blk < own) & (rank < n_sel), 1.0, 0.0)


def _moba_body(*refs):
    n_cast = (len(refs) - _MOBA_INPUTS - 1 - _MOBA_SCRATCH) // 2
    q_ref, k_ref, v_ref, stat_ref, kmean_ref = refs[:_MOBA_INPUTS]
    cast_in = refs[_MOBA_INPUTS:_MOBA_INPUTS + n_cast]
    o_ref = refs[_MOBA_INPUTS + n_cast]
    cast_out = refs[_MOBA_INPUTS + n_cast + 1:_MOBA_INPUTS + 2 * n_cast + 1]
    vt_scr, s_scr, p_scr = refs[-_MOBA_SCRATCH:]
    _cast_slabs(cast_in, cast_out)

    nq = q_ref.shape[0] // ATT_BLOCK
    n_sel = min(MOBA_TOPK, nq - 1)
    bounds = _score_bounds(q_ref, stat_ref)
    first = _channel_is_first(LANES)
    kmean = jnp.concatenate([kmean_ref[j] for j in range(nq)], axis=0)

    def build(i, hd, qts):
        if i not in qts:
            qts[i] = q_ref[_blk(i), :].astype(F32).T
        qh = jnp.where(first, qts[i], 0.0) if hd == 0 else jnp.where(first, 0.0, qts[i])
        allow = None
        if i > n_sel:
            gate = jnp.dot(kmean, qh, preferred_element_type=F32, precision=lax.Precision.HIGHEST)
            allow = _moba_allow(gate, i, n_sel)
        bound = bounds[hd * nq + i:hd * nq + i + 1, _blk(i)] - SCORE_SHIFT
        vt = vt_scr.at[hd * HEAD_DIM:(hd + 1) * HEAD_DIM, :]
        return _Stream(i, qh.astype(BF16), allow, bound, vt, 2 * (i % 2) + hd, hd)

    head_qts = {}
    head = build(nq - 1, 0, head_qts)
    _prestore_scores(head, k_ref, s_scr)

    def make_streams():
        _transpose_values(v_ref, vt_scr)
        qts = dict(head_qts)
        order = [(i, hd) for i in reversed(range(nq)) for hd in range(2)]
        return [head] + [build(i, hd, qts) for i, hd in order[1:]]

    outs = {}

    def on_done(st):
        outs[st.tag] = st.acc * (1.0 / st.l)
        if st.tag == 1:
            o_ref[_blk(st.i), :] = jnp.concatenate([outs[0], outs[1]], axis=0).T.astype(BF16)

    _attention(make_streams, bounds, k_ref, stat_ref, s_scr, p_scr, on_done)


def _moba_attention(mq, mk, mv, stat, kmean, seq, cast_weights, cast_layer):
    t = mq.shape[0]
    nq = seq // ATT_BLOCK
    pairs = MOBA_WIDTH // LANES
    casts = _SideCasts(cast_weights, cast_layer, (t // seq) * pairs, lambda b, p: b * pairs + p)
    blk = pl.BlockSpec((seq, LANES), lambda b, p: (b, p))
    km = pl.BlockSpec((nq, 1, LANES), lambda b, p: (b, 0, p))
    outs = pl.pallas_call(
        _moba_body,
        grid=(t // seq, pairs),
        in_specs=[blk, blk, blk, _stat_spec(nq), km] + casts.in_specs,
        out_specs=[blk] + casts.out_specs,
        out_shape=[jax.ShapeDtypeStruct((t, MOBA_WIDTH), BF16)] + casts.out_shape,
        scratch_shapes=[pltpu.VMEM((LANES, seq), BF16), pltpu.VMEM((4, seq, ATT_BLOCK), F32),
                        pltpu.VMEM((4, seq, ATT_BLOCK), BF16)],
        compiler_params=pltpu.CompilerParams(
            dimension_semantics=("parallel", "parallel"), vmem_limit_bytes=VMEM_LIMIT),
        name="moba_attn",
    )(mq, mk, mv, stat, kmean, *cast_weights)
    return list(outs)


def _inproj_weights(w):
    depth, d, _ = w.shape
    q_scale = HEAD_DIM ** -0.5 * math.log2(math.e)

    def pair_interleave(sec):
        width = sec.shape[-1]
        return sec.reshape(depth, d, width // LANES, 2, 2, QUARTER).swapaxes(3, 4).reshape(depth, d, width)

    parts, base = [], 0
    for width, rotary, scale in ((POOL_WIDTH, False, None),
                                 (DIFF_WIDTH, True, q_scale), (DIFF_WIDTH, True, None), (DIFF_WIDTH, False, None),
                                 (MOBA_WIDTH, True, q_scale), (MOBA_WIDTH, True, None), (MOBA_WIDTH, False, None)):
        sec = w[:, :, base:base + width]
        if rotary:
            sec = pair_interleave(sec)
        if scale is not None:
            sec = sec * scale
        parts.append(sec)
        base += width
    return jnp.concatenate(parts, axis=-1).astype(BF16)


def _rope_tables(seq):
    inv = ROPE_THETA ** (-jnp.arange(0, HEAD_DIM, 2, dtype=F32) / HEAD_DIM)
    ang = jnp.arange(seq, dtype=F32)[:, None] * inv[None, :]
    cos, sin = jnp.cos(ang), jnp.sin(ang)
    return jnp.tile(cos, (1, 4)), jnp.concatenate([-sin, -sin, sin, sin], axis=1)


def kernel(x, ffn1_norm, ffn1_w_in, ffn1_w_out, mix_norm, mix_w_in, mix_w_out, pool_w, pool_scale,
           diff_lambda, diff_subln, ffn2_norm, ffn2_w_in, ffn2_w_out, final_norm):
    batch, seq, d = x.shape
    depth = ffn1_norm.shape[0]
    t = batch * seq
    assert seq % ROW_TILE == 0 and ROW_TILE % MOBA_BLOCK == 0 and ffn1_w_out.shape[1] % FF_CHUNK == 0

    cos, sin = _rope_tables(seq)
    gf = final_norm.reshape(1, d)
    g1, gm, g2 = (g.reshape(depth, 1, d) for g in (ffn1_norm, mix_norm, ffn2_norm))
    w_mix_in = _inproj_weights(mix_w_in)
    groups = len(POOL_WINDOWS)
    pool_bd = jnp.einsum("lgcd,gh->lgchd", pool_w, jnp.eye(groups, dtype=F32)).reshape(depth, POOL_WIDTH, POOL_WIDTH)
    pool_sc = pool_scale.reshape(depth, 1, POOL_WIDTH)
    subln = diff_subln.reshape(depth, 1, LANES)

    w1_in, w1_out = _cast_layer([ffn1_w_in, ffn1_w_out], 0)
    xt = x.reshape(t, d)
    for l in range(depth):
        lambda_init = 0.8 - 0.6 * math.exp(-0.3 * l)
        xt = _ffn(xt, g1, w1_in, w1_out, gf, l, 0, False)
        (u, dq, dk, dv, mq, mk, mv, kmean, dstat, mstat, w2_in, w2_out, w_mix_out) = _inproj(
            xt, gm, w_mix_in, cos, sin, l, seq, [ffn2_w_in, ffn2_w_out, mix_w_out])
        ya = _pool(u, pool_bd, pool_sc, l, seq)
        yb = _diff_attention(dq, dk, dv, dstat, diff_lambda, subln, l, lambda_init, seq)
        next_ffn1 = [ffn1_w_in, ffn1_w_out] if l + 1 < depth else []
        yc, *next_w1 = _moba_attention(mq, mk, mv, mstat, kmean, seq, next_ffn1, l + 1)
        xt = _ffn(xt, g2, w2_in, w2_out, gf, l, 0, l == depth - 1, mix=(ya, yb, yc, w_mix_out))
        if next_w1:
            w1_in, w1_out = next_w1
    return xt.reshape(batch, seq, d)
```

```python
import functools
import math

import jax
import jax.numpy as jnp
from jax import lax
from jax.experimental import pallas as pl
from jax.experimental.pallas import tpu as pltpu

HEAD_DIM = 64
POOL_WINDOWS = (2, 4, 8, 16)
POOL_GROUP_DIM = 64
POOL_WIDTH = 256
DIFF_WIDTH = 512
DIFF_HEADS = 4
MOBA_WIDTH = 256
MOBA_BLOCK = 256
MOBA_TOPK = 3
ROPE_THETA = 10000.0
NORM_EPS = 1e-6
NEG_INF = -1e30

LANES = 128
BF16_ROWS = 16
MXU_COLS = 256
assert POOL_WIDTH == MXU_COLS and MOBA_WIDTH == MXU_COLS and DIFF_WIDTH % MXU_COLS == 0
QUARTER = HEAD_DIM // 2
ATT_BLOCK = 256
ROW_TILE = 1024
FF_CHUNK = 256
CAST_STEPS = 16
ATT_HEADS_PER_STEP = 2
SCORE_SHIFT = 40.0
SAFE_BOUND = 70.0
VALUE_LIMIT = 2.0 ** 60
VMEM_LIMIT = 56 * 1024 * 1024

F32 = jnp.float32
BF16 = jnp.bfloat16


def _dot(a, b):
    return jnp.dot(a, b, preferred_element_type=F32)


def _rms(x, g):
    return x * lax.rsqrt(jnp.mean(x * x, axis=-1, keepdims=True) + NORM_EPS) * g


def _resident(shape):
    nd = len(shape)
    return pl.BlockSpec(shape, lambda *_: (0,) * nd, pipeline_mode=pl.Buffered(1))


def _layer(arr, layer):
    tail = (0,) * (arr.ndim - 1)
    return pl.BlockSpec((None,) + arr.shape[1:], lambda *_: (layer,) + tail, pipeline_mode=pl.Buffered(1))


class _SideCasts:
    def __init__(self, weights, layer, steps, step_of):
        self.in_specs, self.out_shape, self.out_specs = [], [], []
        for w in weights:
            _, a, b = w.shape
            col_parts = next(c for c in range(1, steps + 1) if steps % c == 0 and b % (c * LANES) == 0
                             and a % ((steps // c) * BF16_ROWS) == 0)
            slab = (a // (steps // col_parts), b // col_parts)

            def index(*g, _layer=layer, _cp=col_parts):
                step = step_of(*g)
                return (_layer, step // _cp, step % _cp)

            self.in_specs.append(pl.BlockSpec((None,) + slab, index))
            self.out_shape.append(jax.ShapeDtypeStruct((1, a, b), BF16))
            self.out_specs.append(pl.BlockSpec((None,) + slab, functools.partial(index, _layer=0)))


def _cast_slabs(in_refs, out_refs):
    for src, dst in zip(in_refs, out_refs):
        dst[...] = src[...].astype(BF16)


def _cast_body(*refs):
    _cast_slabs(refs[:len(refs) // 2], refs[len(refs) // 2:])


def _cast_layer(weights, layer):
    casts = _SideCasts(weights, layer, CAST_STEPS, lambda i: i)
    return pl.pallas_call(
        _cast_body,
        grid=(CAST_STEPS,),
        in_specs=casts.in_specs,
        out_specs=casts.out_specs,
        out_shape=casts.out_shape,
        compiler_params=pltpu.CompilerParams(dimension_semantics=("parallel",), vmem_limit_bytes=VMEM_LIMIT),
        name="cast",
    )(*weights)


def _ffn_body(x_ref, g_ref, win_ref, wout_ref, gf_ref, *rest, d_ff, final_norm, with_mix):
    if with_mix:
        ya_ref, yb_ref, yc_ref, wmix_ref, o_ref, u_scr = rest
        lo_b, lo_c = POOL_WIDTH, POOL_WIDTH + DIFF_WIDTH
        x = (x_ref[...] + _dot(ya_ref[...], wmix_ref[0:lo_b, :]) + _dot(yb_ref[...], wmix_ref[lo_b:lo_c, :])
             + _dot(yc_ref[...], wmix_ref[lo_c:, :]))
    else:
        o_ref, u_scr = rest
        x = x_ref[...]
    h = _rms(x, g_ref[...]).astype(BF16)
    for c in range(d_ff // FF_CHUNK):
        lo = c * FF_CHUNK
        a = _dot(h, win_ref[:, lo:lo + FF_CHUNK])
        b = _dot(h, win_ref[:, d_ff + lo:d_ff + lo + FF_CHUNK])
        u_scr[:, lo:lo + FF_CHUNK] = (a / (1.0 + jnp.exp(-a)) * b).astype(BF16)
    y = x + 0.5 * _dot(u_scr[...], wout_ref[...])
    if final_norm:
        y = _rms(y, gf_ref[...])
    o_ref[...] = y


def _ffn(x, g, w_in, w_out, gf, g_layer, w_layer, final_norm, mix=None):
    t, d = x.shape
    d_ff = w_out.shape[1]

    def rows(width):
        return pl.BlockSpec((ROW_TILE, width), lambda i: (i, 0))

    args = [x, g, w_in, w_out, gf]
    in_specs = [rows(d), _layer(g, g_layer), _layer(w_in, w_layer), _layer(w_out, w_layer), _resident((1, d))]
    if mix is not None:
        args.extend(mix)
        in_specs.extend([rows(POOL_WIDTH), rows(DIFF_WIDTH), rows(MOBA_WIDTH), _layer(mix[3], w_layer)])
    return pl.pallas_call(
        functools.partial(_ffn_body, d_ff=d_ff, final_norm=final_norm, with_mix=mix is not None),
        grid=(t // ROW_TILE,),
        in_specs=in_specs,
        out_specs=rows(d),
        out_shape=jax.ShapeDtypeStruct((t, d), F32),
        scratch_shapes=[pltpu.VMEM((ROW_TILE, d_ff), BF16)],
        compiler_params=pltpu.CompilerParams(dimension_semantics=("parallel",), vmem_limit_bytes=VMEM_LIMIT),
        name="mix_ffn" if mix is not None else "ffn",
    )(*args)


_INPROJ_INPUTS = 5
_INPROJ_OUTPUTS = 10


def _inproj_body(*refs):
    n_cast = (len(refs) - _INPROJ_INPUTS - _INPROJ_OUTPUTS) // 2
    x_ref, g_ref, w_ref, cos_ref, sin_ref = refs[:_INPROJ_INPUTS]
    cast_in = refs[_INPROJ_INPUTS:_INPROJ_INPUTS + n_cast]
    outs = refs[_INPROJ_INPUTS + n_cast:]
    u_ref, dq_ref, dk_ref, dv_ref, mq_ref, mk_ref, mv_ref, kmean_ref, dstat_ref, mstat_ref = outs[:_INPROJ_OUTPUTS]
    _cast_slabs(cast_in, outs[_INPROJ_OUTPUTS:])

    h = _rms(x_ref[...], g_ref[...]).astype(BF16)
    cos = cos_ref[...]
    sin = sin_ref[...]

    def proj(lo):
        return _dot(h, w_ref[:, lo:lo + MXU_COLS])

    def rope(z):
        halves = [z[:, c * LANES:(c + 1) * LANES] for c in range(MXU_COLS // LANES)]
        return jnp.concatenate([zc * cos + pltpu.roll(zc, 2 * QUARTER, axis=1) * sin for zc in halves], axis=1)

    def put_absmax(stat_ref, row, z, lo):
        for r in range(ROW_TILE // ATT_BLOCK):
            m = jnp.max(jnp.abs(z[r * ATT_BLOCK:(r + 1) * ATT_BLOCK].astype(F32)), axis=0, keepdims=True)
            stat_ref[r, row:row + 1, lo:lo + MXU_COLS] = m

    u_ref[...] = proj(0)
    base = POOL_WIDTH
    for c in range(DIFF_WIDTH // MXU_COLS):
        dq_ref[:, c * MXU_COLS:(c + 1) * MXU_COLS] = rope(proj(base + c * MXU_COLS)).astype(BF16)
    base += DIFF_WIDTH
    for ref, row, rotary in ((dk_ref, 0, True), (dv_ref, 1, False)):
        for c in range(DIFF_WIDTH // MXU_COLS):
            z = proj(base + c * MXU_COLS)
            z = (rope(z) if rotary else z).astype(BF16)
            ref[:, c * MXU_COLS:(c + 1) * MXU_COLS] = z
            put_absmax(dstat_ref, row, z, c * MXU_COLS)
        base += DIFF_WIDTH
    mq_ref[...] = rope(proj(base)).astype(BF16)
    base += MOBA_WIDTH
    k = rope(proj(base))
    mk_ref[...] = k.astype(BF16)
    put_absmax(mstat_ref, 0, k.astype(BF16), 0)
    for r in range(ROW_TILE // MOBA_BLOCK):
        kmean_ref[r] = jnp.mean(k[r * MOBA_BLOCK:(r + 1) * MOBA_BLOCK], axis=0, keepdims=True)
    base += MOBA_WIDTH
    v = proj(base).astype(BF16)
    mv_ref[...] = v
    put_absmax(mstat_ref, 1, v, 0)


def _inproj(x, g, w, cos, sin, layer, seq, cast_weights):
    t, d = x.shape
    nt = t // ROW_TILE
    tiles_per_seq = seq // ROW_TILE
    blocks_per_tile = ROW_TILE // MOBA_BLOCK
    casts = _SideCasts(cast_weights, layer, nt, lambda i: i)

    def rows(width):
        return pl.BlockSpec((ROW_TILE, width), lambda i: (i, 0))

    table = pl.BlockSpec((ROW_TILE, LANES), lambda i: (i % tiles_per_seq, 0))
    widths = (POOL_WIDTH, DIFF_WIDTH, DIFF_WIDTH, DIFF_WIDTH, MOBA_WIDTH, MOBA_WIDTH, MOBA_WIDTH)
    dtypes = (F32, BF16, BF16, BF16, BF16, BF16, BF16)
    out_shape = [jax.ShapeDtypeStruct((t, wd), dt) for wd, dt in zip(widths, dtypes)]
    out_shape.append(jax.ShapeDtypeStruct((t // MOBA_BLOCK, 1, MOBA_WIDTH), F32))
    out_specs = [rows(wd) for wd in widths]
    out_specs.append(pl.BlockSpec((blocks_per_tile, 1, MOBA_WIDTH), lambda i: (i, 0, 0)))
    for wd in (DIFF_WIDTH, MOBA_WIDTH):
        out_shape.append(jax.ShapeDtypeStruct((t // ATT_BLOCK, 2, wd), F32))
        out_specs.append(pl.BlockSpec((blocks_per_tile, 2, wd), lambda i: (i, 0, 0)))
    assert len(out_shape) == _INPROJ_OUTPUTS
    outs = pl.pallas_call(
        _inproj_body,
        grid=(nt,),
        in_specs=[rows(d), _layer(g, layer), _layer(w, layer), table, table] + casts.in_specs,
        out_specs=out_specs + casts.out_specs,
        out_shape=out_shape + casts.out_shape,
        compiler_params=pltpu.CompilerParams(dimension_semantics=("parallel",), vmem_limit_bytes=VMEM_LIMIT),
        name="inproj",
    )(x, g, w, cos, sin, *cast_weights)
    return list(outs)


def _pool_body(u_ref, w_ref, scale_ref, o_ref):
    s = u_ref.shape[0]
    row = lax.broadcasted_iota(jnp.int32, (s, 1), 0)
    groups_per_tile = LANES // POOL_GROUP_DIM
    lane_group = lax.broadcasted_iota(jnp.int32, (1, LANES), 1) // POOL_GROUP_DIM
    tiles = []
    for t in range(POOL_WIDTH // LANES):
        u = u_ref[:, t * LANES:(t + 1) * LANES]
        window_sum, shift, pooled = u, 1, None
        for gi in range(groups_per_tile):
            wdw = POOL_WINDOWS[t * groups_per_tile + gi]
            while shift < wdw:
                shifted = jnp.where(row >= shift, pltpu.roll(window_sum, shift, axis=0), 0.0)
                window_sum = window_sum + shifted
                shift *= 2
            mean = window_sum * (1.0 / jnp.minimum(row + 1, wdw).astype(F32))
            pooled = mean if pooled is None else jnp.where(lane_group == gi, mean, pooled)
        tiles.append(pooled - u)
    mixed = _dot(jnp.concatenate(tiles, axis=1), w_ref[...])
    o_ref[...] = (mixed * scale_ref[...]).astype(BF16)


def _pool(u, w_blockdiag, scale, layer, seq):
    t = u.shape[0]
    blk = pl.BlockSpec((seq, POOL_WIDTH), lambda b: (b, 0))
    return pl.pallas_call(
        _pool_body,
        grid=(t // seq,),
        in_specs=[blk, _layer(w_blockdiag, layer), _layer(scale, layer)],
        out_specs=blk,
        out_shape=jax.ShapeDtypeStruct((t, POOL_WIDTH), BF16),
        compiler_params=pltpu.CompilerParams(dimension_semantics=("parallel",), vmem_limit_bytes=VMEM_LIMIT),
        name="pool",
    )(u, w_blockdiag, scale)


def _channel_is_first(n):
    ch = lax.broadcasted_iota(jnp.int32, (n, 1), 0)
    return (ch // QUARTER) % 2 == 0


def _blk(j):
    return slice(j * ATT_BLOCK, (j + 1) * ATT_BLOCK)


def _transpose_values(v_ref, vt_scr):
    for j in range(v_ref.shape[0] // ATT_BLOCK):
        vt_scr[:, _blk(j)] = v_ref[_blk(j), :].astype(F32).T.astype(BF16)


def _causal_mask():
    key = lax.broadcasted_iota(jnp.int32, (ATT_BLOCK, ATT_BLOCK), 0)
    qry = lax.broadcasted_iota(jnp.int32, (ATT_BLOCK, ATT_BLOCK), 1)
    return key <= qry


def _score_bounds(q_ref, stat_ref):
    nq = stat_ref.shape[0]
    rows, run = [], None
    for j in range(nq):
        m = stat_ref[j, 0:1, :]
        run = m if run is None else jnp.maximum(run, m)
        rows.append(run)
    kabs = jnp.concatenate(rows, axis=0)
    lane_first = (lax.broadcasted_iota(jnp.int32, (1, LANES), 1) // QUARTER) % 2 == 0
    ka = jnp.concatenate([jnp.where(lane_first, kabs, 0.0), jnp.where(lane_first, 0.0, kabs)], axis=0)
    b = lax.dot_general(ka.astype(BF16), jnp.abs(q_ref[...]), (((1,), (1,)), ((), ())), preferred_element_type=F32)
    return b * (1.0 + 2.0 ** -10) + 2.0 ** -10


class _Stream:
    def __init__(self, i, qt, allow, bound, vt, slot, tag):
        self.i, self.qt, self.allow, self.bound, self.vt, self.slot, self.tag = i, qt, allow, bound, vt, slot, tag
        self.mx = self.l = self.acc = None
        self.prestored = False


def _masked_scores(st, j, k_ref, s_scr, causal):
    if st.prestored:
        return s_scr[st.slot, _blk(j), :]
    s = _dot(k_ref[_blk(j), :], st.qt)
    if j == st.i:
        s = jnp.where(causal, s, NEG_INF)
    elif st.allow is not None:
        s = jnp.where(st.allow[j:j + 1, :] > 0.5, s, NEG_INF)
    return s


def _prestore_scores(st, k_ref, s_scr):
    causal = _causal_mask()
    for j in range(st.i + 1):
        s_scr[st.slot, _blk(j), :] = _masked_scores(st, j, k_ref, s_scr, causal)
    st.prestored = True


def _score_phase(st, k_ref, s_scr, causal):
    mx = None
    for j in range(st.i + 1):
        s = _masked_scores(st, j, k_ref, s_scr, causal)
        if not st.prestored:
            s_scr[st.slot, _blk(j), :] = s
        bm = jnp.max(s, axis=0, keepdims=True)
        mx = bm if mx is None else jnp.maximum(mx, bm)
        yield
    st.mx = mx


def _exp_phase(st, s_scr, p_scr):
    for j in range(st.i + 1):
        _emit_probabilities(st, j, jnp.exp2(s_scr[st.slot, _blk(j), :] - st.mx), p_scr)
        yield


def _bounded_phase(st, k_ref, s_scr, p_scr, causal):
    for j in range(st.i + 1):
        _emit_probabilities(st, j, jnp.exp2(_masked_scores(st, j, k_ref, s_scr, causal) - st.bound), p_scr)
        yield


def _emit_probabilities(st, j, p, p_scr):
    p_scr[st.slot, _blk(j), :] = p.astype(BF16)
    bl = jnp.sum(p, axis=0, keepdims=True)
    st.l = bl if j == 0 else st.l + bl


def _value_phase(st, p_scr):
    n = (st.i + 1) * ATT_BLOCK
    st.acc = _dot(st.vt[:, 0:n], p_scr[st.slot, 0:n, :])
    yield


def _run_streams(streams, k_ref, s_scr, p_scr, causal, on_done, bounded):
    def stages(st):
        if bounded:
            return [_bounded_phase(st, k_ref, s_scr, p_scr, causal), _value_phase(st, p_scr)]
        return [_score_phase(st, k_ref, s_scr, causal), _exp_phase(st, s_scr, p_scr), _value_phase(st, p_scr)]

    depth = 2 if bounded else 3
    staged = [stages(st) for st in streams]
    for tick in range(len(streams) + depth - 1):
        live = [staged[tick - d][d] for d in range(depth) if 0 <= tick - d < len(streams)]
        while live:
            for g in list(live):
                if next(g, _DONE) is _DONE:
                    live.remove(g)
        if tick >= depth - 1:
            on_done(streams[tick - depth + 1])


_DONE = object()


def _attention(make_streams, bounds, k_ref, stat_ref, s_scr, p_scr, on_done):
    nq = bounds.shape[0] // 2
    worst = jnp.maximum(bounds[nq - 1:nq, :], bounds[2 * nq - 1:2 * nq, :])
    v_abs = jnp.max(functools.reduce(jnp.maximum, [stat_ref[j, 1:2, :] for j in range(nq)]))
    safe = jnp.logical_and(jnp.max(worst) <= SAFE_BOUND, v_abs <= VALUE_LIMIT)

    @pl.when(safe)
    def _():
        _run_streams(make_streams(), k_ref, s_scr, p_scr, _causal_mask(), on_done, bounded=True)

    @pl.when(jnp.logical_not(safe))
    def _():
        _run_streams(make_streams(), k_ref, s_scr, p_scr, _causal_mask(), on_done, bounded=False)


def _lane_views(refs, part):
    return [r.at[(slice(None),) * (len(r.shape) - 1) + (pl.ds(part * LANES, LANES),)] for r in refs]


def _diff_body(q_ref, k_ref, v_ref, stat_ref, lam_ref, g_ref, o_ref, vt_scr, s_scr, p_scr, *, lambda_init):
    lam = lam_ref[...]
    lam_val = (jnp.exp(jnp.sum(lam[0:1] * lam[1:2], keepdims=True))
               - jnp.exp(jnp.sum(lam[2:3] * lam[3:4], keepdims=True)) + lambda_init)
    gain = g_ref[...] * (1.0 - lambda_init)
    for part in range(q_ref.shape[1] // LANES):
        q_h, k_h, v_h, stat_h, o_h = _lane_views([q_ref, k_ref, v_ref, stat_ref, o_ref], part)
        _diff_head(q_h, k_h, v_h, stat_h, lam_val, gain, o_h, vt_scr, s_scr, p_scr)


def _diff_head(q_ref, k_ref, v_ref, stat_ref, lam_val, gain, o_ref, vt_scr, s_scr, p_scr):
    nq = q_ref.shape[0] // ATT_BLOCK
    bounds = _score_bounds(q_ref, stat_ref)
    first = _channel_is_first(LANES)

    def build(i, mi, qts):
        if i not in qts:
            qts[i] = q_ref[_blk(i), :].astype(F32).T
        qm = jnp.where(first, qts[i], 0.0) if mi == 0 else jnp.where(first, 0.0, qts[i])
        bound = bounds[mi * nq + i:mi * nq + i + 1, _blk(i)] - SCORE_SHIFT
        return _Stream(i, qm.astype(BF16), None, bound, vt_scr, 2 * (i % 2) + mi, mi)

    head_qts = {}
    head = build(nq - 1, 0, head_qts)
    _prestore_scores(head, k_ref, s_scr)

    def make_streams():
        _transpose_values(v_ref, vt_scr)
        qts = dict(head_qts)
        order = [(i, mi) for i in reversed(range(nq)) for mi in range(2)]
        return [head] + [build(i, mi, qts) for i, mi in order[1:]]

    outs = {}

    def on_done(st):
        outs[st.tag] = st.acc * (1.0 / st.l)
        if st.tag == 1:
            ot = outs[0] - lam_val * outs[1]
            o_ref[_blk(st.i), :] = _rms(ot.T, gain).astype(BF16)

    _attention(make_streams, bounds, k_ref, stat_ref, s_scr, p_scr, on_done)


def _stat_spec(blocks, width):
    return pl.BlockSpec((blocks, 2, width), lambda b, h: (b, 0, h))


def _diff_attention(dq, dk, dv, stat, lam, subln, layer, lambda_init, seq):
    t = dq.shape[0]
    width = ATT_HEADS_PER_STEP * LANES
    blk = pl.BlockSpec((seq, width), lambda b, h: (b, h))
    return pl.pallas_call(
        functools.partial(_diff_body, lambda_init=lambda_init),
        grid=(t // seq, DIFF_WIDTH // width),
        in_specs=[blk, blk, blk, _stat_spec(seq // ATT_BLOCK, width), _layer(lam, layer), _layer(subln, layer)],
        out_specs=blk,
        out_shape=jax.ShapeDtypeStruct((t, DIFF_WIDTH), BF16),
        scratch_shapes=[pltpu.VMEM((LANES, seq), BF16), pltpu.VMEM((4, seq, ATT_BLOCK), F32),
                        pltpu.VMEM((4, seq, ATT_BLOCK), BF16)],
        compiler_params=pltpu.CompilerParams(
            dimension_semantics=("parallel", "parallel"), vmem_limit_bytes=VMEM_LIMIT),
        name="diff_attn",
    )(dq, dk, dv, stat, lam, subln)


_MOBA_INPUTS = 5
_MOBA_SCRATCH = 3


def _moba_allow(gate, own, n_sel):
    nb = gate.shape[0]
    blk = lax.broadcasted_iota(jnp.int32, (nb, 1), 0)
    rank = jnp.zeros(gate.shape, jnp.int32)
    for other in range(own):
        g_other = gate[other:other + 1, :]
        beats = (g_other > gate) | ((g_other == gate) & (other < blk))
        rank = rank + jnp.where(beats, 1, 0)
    return jnp.where((blk < own) & (rank < n_sel), 1.0, 0.0)


def _moba_body(*refs):
    n_cast = (len(refs) - _MOBA_INPUTS - 1 - _MOBA_SCRATCH) // 2
    q_ref, k_ref, v_ref, stat_ref, kmean_ref = refs[:_MOBA_INPUTS]
    cast_in = refs[_MOBA_INPUTS:_MOBA_INPUTS + n_cast]
    o_ref = refs[_MOBA_INPUTS + n_cast]
    cast_out = refs[_MOBA_INPUTS + n_cast + 1:_MOBA_INPUTS + 2 * n_cast + 1]
    vt_scr, s_scr, p_scr = refs[-_MOBA_SCRATCH:]
    _cast_slabs(cast_in, cast_out)
    for part in range(q_ref.shape[1] // LANES):
        _moba_pair(*_lane_views([q_ref, k_ref, v_ref, stat_ref, kmean_ref, o_ref], part), vt_scr, s_scr, p_scr)


def _moba_pair(q_ref, k_ref, v_ref, stat_ref, kmean_ref, o_ref, vt_scr, s_scr, p_scr):
    nq = q_ref.shape[0] // ATT_BLOCK
    n_sel = min(MOBA_TOPK, nq - 1)
    bounds = _score_bounds(q_ref, stat_ref)
    first = _channel_is_first(LANES)
    kmean = jnp.concatenate([kmean_ref[j] for j in range(nq)], axis=0)

    def build(i, hd, qts):
        if i not in qts:
            qts[i] = q_ref[_blk(i), :].astype(F32).T
        qh = jnp.where(first, qts[i], 0.0) if hd == 0 else jnp.where(first, 0.0, qts[i])
        allow = None
        if i > n_sel:
            gate = jnp.dot(kmean, qh, preferred_element_type=F32, precision=lax.Precision.HIGHEST)
            allow = _moba_allow(gate, i, n_sel)
        bound = bounds[hd * nq + i:hd * nq + i + 1, _blk(i)] - SCORE_SHIFT
        vt = vt_scr.at[hd * HEAD_DIM:(hd + 1) * HEAD_DIM, :]
        return _Stream(i, qh.astype(BF16), allow, bound, vt, 2 * (i % 2) + hd, hd)

    head_qts = {}
    head = build(nq - 1, 0, head_qts)
    _prestore_scores(head, k_ref, s_scr)

    def make_streams():
        _transpose_values(v_ref, vt_scr)
        qts = dict(head_qts)
        order = [(i, hd) for i in reversed(range(nq)) for hd in range(2)]
        return [head] + [build(i, hd, qts) for i, hd in order[1:]]

    outs = {}

    def on_done(st):
        outs[st.tag] = st.acc * (1.0 / st.l)
        if st.tag == 1:
            o_ref[_blk(st.i), :] = jnp.concatenate([outs[0], outs[1]], axis=0).T.astype(BF16)

    _attention(make_streams, bounds, k_ref, stat_ref, s_scr, p_scr, on_done)


def _moba_attention(mq, mk, mv, stat, kmean, seq, cast_weights, cast_layer):
    t = mq.shape[0]
    nq = seq // ATT_BLOCK
    width = ATT_HEADS_PER_STEP * LANES
    pairs = MOBA_WIDTH // width
    casts = _SideCasts(cast_weights, cast_layer, (t // seq) * pairs, lambda b, p: b * pairs + p)
    blk = pl.BlockSpec((seq, width), lambda b, p: (b, p))
    km = pl.BlockSpec((nq, 1, width), lambda b, p: (b, 0, p))
    outs = pl.pallas_call(
        _moba_body,
        grid=(t // seq, pairs),
        in_specs=[blk, blk, blk, _stat_spec(nq, width), km] + casts.in_specs,
        out_specs=[blk] + casts.out_specs,
        out_shape=[jax.ShapeDtypeStruct((t, MOBA_WIDTH), BF16)] + casts.out_shape,
        scratch_shapes=[pltpu.VMEM((LANES, seq), BF16), pltpu.VMEM((4, seq, ATT_BLOCK), F32),
                        pltpu.VMEM((4, seq, ATT_BLOCK), BF16)],
        compiler_params=pltpu.CompilerParams(
            dimension_semantics=("parallel", "parallel"), vmem_limit_bytes=VMEM_LIMIT),
        name="moba_attn",
    )(mq, mk, mv, stat, kmean, *cast_weights)
    return list(outs)


def _inproj_weights(w):
    depth, d, _ = w.shape
    q_scale = HEAD_DIM ** -0.5 * math.log2(math.e)

    def pair_interleave(sec):
        width = sec.shape[-1]
        return sec.reshape(depth, d, width // LANES, 2, 2, QUARTER).swapaxes(3, 4).reshape(depth, d, width)

    parts, base = [], 0
    for width, rotary, scale in ((POOL_WIDTH, False, None),
                                 (DIFF_WIDTH, True, q_scale), (DIFF_WIDTH, True, None), (DIFF_WIDTH, False, None),
                                 (MOBA_WIDTH, True, q_scale), (MOBA_WIDTH, True, None), (MOBA_WIDTH, False, None)):
        sec = w[:, :, base:base + width]
        if rotary:
            sec = pair_interleave(sec)
        if scale is not None:
            sec = sec * scale
        parts.append(sec)
        base += width
    return jnp.concatenate(parts, axis=-1).astype(BF16)


def _rope_tables(seq):
    inv = ROPE_THETA ** (-jnp.arange(0, HEAD_DIM, 2, dtype=F32) / HEAD_DIM)
    ang = jnp.arange(seq, dtype=F32)[:, None] * inv[None, :]
    cos, sin = jnp.cos(ang), jnp.sin(ang)
    return jnp.tile(cos, (1, 4)), jnp.concatenate([-sin, -sin, sin, sin], axis=1)


def kernel(x, ffn1_norm, ffn1_w_in, ffn1_w_out, mix_norm, mix_w_in, mix_w_out, pool_w, pool_scale,
           diff_lambda, diff_subln, ffn2_norm, ffn2_w_in, ffn2_w_out, final_norm):
    batch, seq, d = x.shape
    depth = ffn1_norm.shape[0]
    t = batch * seq
    assert seq % ROW_TILE == 0 and ROW_TILE % MOBA_BLOCK == 0 and ffn1_w_out.shape[1] % FF_CHUNK == 0

    cos, sin = _rope_tables(seq)
    gf = final_norm.reshape(1, d)
    g1, gm, g2 = (g.reshape(depth, 1, d) for g in (ffn1_norm, mix_norm, ffn2_norm))
    w_mix_in = _inproj_weights(mix_w_in)
    groups = len(POOL_WINDOWS)
    pool_bd = jnp.einsum("lgcd,gh->lgchd", pool_w, jnp.eye(groups, dtype=F32)).reshape(depth, POOL_WIDTH, POOL_WIDTH)
    pool_sc = pool_scale.reshape(depth, 1, POOL_WIDTH)
    subln = diff_subln.reshape(depth, 1, LANES)

    w1_in, w1_out = _cast_layer([ffn1_w_in, ffn1_w_out], 0)
    xt = x.reshape(t, d)
    for l in range(depth):
        lambda_init = 0.8 - 0.6 * math.exp(-0.3 * l)
        xt = _ffn(xt, g1, w1_in, w1_out, gf, l, 0, False)
        (u, dq, dk, dv, mq, mk, mv, kmean, dstat, mstat, w2_in, w2_out, w_mix_out) = _inproj(
            xt, gm, w_mix_in, cos, sin, l, seq, [ffn2_w_in, ffn2_w_out, mix_w_out])
        ya = _pool(u, pool_bd, pool_sc, l, seq)
        yb = _diff_attention(dq, dk, dv, dstat, diff_lambda, subln, l, lambda_init, seq)
        next_ffn1 = [ffn1_w_in, ffn1_w_out] if l + 1 < depth else []
        yc, *next_w1 = _moba_attention(mq, mk, mv, mstat, kmean, seq, next_ffn1, l + 1)
        xt = _ffn(xt, g2, w2_in, w2_out, gf, l, 0, l == depth - 1, mix=(ya, yb, yc, w_mix_out))
        if next_w1:
            w1_in, w1_out = next_w1
    return xt.reshape(batch, seq, d)
```

```python
import functools
import math

import jax
import jax.numpy as jnp
from jax import lax
from jax.experimental import pallas as pl
from jax.experimental.pallas import tpu as pltpu

HEAD_DIM = 64
POOL_WINDOWS = (2, 4, 8, 16)
POOL_GROUP_DIM = 64
POOL_WIDTH = 256
DIFF_WIDTH = 512
DIFF_HEADS = 4
MOBA_WIDTH = 256
MOBA_BLOCK = 256
MOBA_TOPK = 3
ROPE_THETA = 10000.0
NORM_EPS = 1e-6
NEG_INF = -1e30

LANES = 128
BF16_ROWS = 16
MXU_COLS = 256
assert POOL_WIDTH == MXU_COLS and MOBA_WIDTH == MXU_COLS and DIFF_WIDTH % MXU_COLS == 0
QUARTER = HEAD_DIM // 2
ATT_BLOCK = 256
ROW_TILE = 1024
FUSED_TILE = 512
FF_CHUNK = 256
CAST_STEPS = 16
SCORE_SHIFT = 40.0
SAFE_BOUND = 70.0
VALUE_LIMIT = 2.0 ** 60
VMEM_LIMIT = 56 * 1024 * 1024

F32 = jnp.float32
BF16 = jnp.bfloat16


def _dot(a, b):
    return jnp.dot(a, b, preferred_element_type=F32)


def _rms(x, g):
    return x * lax.rsqrt(jnp.mean(x * x, axis=-1, keepdims=True) + NORM_EPS) * g


def _resident(shape):
    nd = len(shape)
    return pl.BlockSpec(shape, lambda *_: (0,) * nd, pipeline_mode=pl.Buffered(1))


def _layer(arr, layer):
    tail = (0,) * (arr.ndim - 1)
    return pl.BlockSpec((None,) + arr.shape[1:], lambda *_: (layer,) + tail, pipeline_mode=pl.Buffered(1))


class _SideCasts:
    def __init__(self, weights, layer, steps, step_of):
        self.in_specs, self.out_shape, self.out_specs = [], [], []
        for w in weights:
            _, a, b = w.shape
            col_parts = next(c for c in range(1, steps + 1) if steps % c == 0 and b % (c * LANES) == 0
                             and a % ((steps // c) * BF16_ROWS) == 0)
            slab = (a // (steps // col_parts), b // col_parts)

            def index(*g, _layer=layer, _cp=col_parts):
                step = step_of(*g)
                return (_layer, step // _cp, step % _cp)

            self.in_specs.append(pl.BlockSpec((None,) + slab, index))
            self.out_shape.append(jax.ShapeDtypeStruct((1, a, b), BF16))
            self.out_specs.append(pl.BlockSpec((None,) + slab, functools.partial(index, _layer=0)))


def _cast_slabs(in_refs, out_refs):
    for src, dst in zip(in_refs, out_refs):
        dst[...] = src[...].astype(BF16)


def _cast_body(*refs):
    _cast_slabs(refs[:len(refs) // 2], refs[len(refs) // 2:])


def _cast_layer(weights, layer):
    casts = _SideCasts(weights, layer, CAST_STEPS, lambda i: i)
    return pl.pallas_call(
        _cast_body,
        grid=(CAST_STEPS,),
        in_specs=casts.in_specs,
        out_specs=casts.out_specs,
        out_shape=casts.out_shape,
        compiler_params=pltpu.CompilerParams(dimension_semantics=("parallel",), vmem_limit_bytes=VMEM_LIMIT),
        name="cast",
    )(*weights)


def _ffn_body(x_ref, g_ref, win_ref, wout_ref, gf_ref, *rest, d_ff, final_norm, with_mix):
    if with_mix:
        ya_ref, yb_ref, yc_ref, wmix_ref, o_ref, u_scr = rest
        lo_b, lo_c = POOL_WIDTH, POOL_WIDTH + DIFF_WIDTH
        x = (x_ref[...] + _dot(ya_ref[...], wmix_ref[0:lo_b, :]) + _dot(yb_ref[...], wmix_ref[lo_b:lo_c, :])
             + _dot(yc_ref[...], wmix_ref[lo_c:, :]))
    else:
        o_ref, u_scr = rest
        x = x_ref[...]
    y = _ffn_rows(x, g_ref, win_ref, wout_ref, u_scr)
    if final_norm:
        y = _rms(y, gf_ref[...])
    o_ref[...] = y


def _ffn_rows(x, g_ref, win_ref, wout_ref, u_scr):
    d_ff = wout_ref.shape[0]
    h = _rms(x, g_ref[...]).astype(BF16)
    for c in range(d_ff // FF_CHUNK):
        lo = c * FF_CHUNK
        a = _dot(h, win_ref[:, lo:lo + FF_CHUNK])
        b = _dot(h, win_ref[:, d_ff + lo:d_ff + lo + FF_CHUNK])
        u_scr[:, lo:lo + FF_CHUNK] = (a / (1.0 + jnp.exp(-a)) * b).astype(BF16)
    return x + 0.5 * _dot(u_scr[...], wout_ref[...])


def _ffn(x, g, w_in, w_out, gf, g_layer, w_layer, final_norm, mix=None):
    t, d = x.shape
    d_ff = w_out.shape[1]

    def rows(width):
        return pl.BlockSpec((ROW_TILE, width), lambda i: (i, 0))

    args = [x, g, w_in, w_out, gf]
    in_specs = [rows(d), _layer(g, g_layer), _layer(w_in, w_layer), _layer(w_out, w_layer), _resident((1, d))]
    if mix is not None:
        args.extend(mix)
        in_specs.extend([rows(POOL_WIDTH), rows(DIFF_WIDTH), rows(MOBA_WIDTH), _layer(mix[3], w_layer)])
    return pl.pallas_call(
        functools.partial(_ffn_body, d_ff=d_ff, final_norm=final_norm, with_mix=mix is not None),
        grid=(t // ROW_TILE,),
        in_specs=in_specs,
        out_specs=rows(d),
        out_shape=jax.ShapeDtypeStruct((t, d), F32),
        scratch_shapes=[pltpu.VMEM((ROW_TILE, d_ff), BF16)],
        compiler_params=pltpu.CompilerParams(dimension_semantics=("parallel",), vmem_limit_bytes=VMEM_LIMIT),
        name="mix_ffn" if mix is not None else "ffn",
    )(*args)


_INPROJ_INPUTS = 5
_INPROJ_OUTPUTS = 10


def _inproj_body(*refs, fused):
    n_in = _INPROJ_INPUTS + (3 if fused else 0)
    n_out = _INPROJ_OUTPUTS + (1 if fused else 0)
    n_scr = 1 if fused else 0
    n_cast = (len(refs) - n_in - n_out - n_scr) // 2
    x_ref, g_ref, w_ref, cos_ref, sin_ref = refs[:_INPROJ_INPUTS]
    cast_in = refs[n_in:n_in + n_cast]
    outs = refs[n_in + n_cast:len(refs) - n_scr]
    u_ref, dq_ref, dk_ref, dv_ref, mq_ref, mk_ref, mv_ref, kmean_ref, dstat_ref, mstat_ref = outs[:_INPROJ_OUTPUTS]
    _cast_slabs(cast_in, outs[n_out:])

    x = x_ref[...]
    if fused:
        g1_ref, win_ref, wout_ref = refs[_INPROJ_INPUTS:n_in]
        x = _ffn_rows(x, g1_ref, win_ref, wout_ref, refs[-1])
        outs[_INPROJ_OUTPUTS][...] = x
    tile = x.shape[0]
    h = _rms(x, g_ref[...]).astype(BF16)
    cos = cos_ref[...]
    sin = sin_ref[...]

    def proj(lo):
        return _dot(h, w_ref[:, lo:lo + MXU_COLS])

    def rope(z):
        halves = [z[:, c * LANES:(c + 1) * LANES] for c in range(MXU_COLS // LANES)]
        return jnp.concatenate([zc * cos + pltpu.roll(zc, 2 * QUARTER, axis=1) * sin for zc in halves], axis=1)

    def put_absmax(stat_ref, row, z, lo):
        for r in range(tile // ATT_BLOCK):
            m = jnp.max(jnp.abs(z[r * ATT_BLOCK:(r + 1) * ATT_BLOCK].astype(F32)), axis=0, keepdims=True)
            stat_ref[r, row:row + 1, lo:lo + MXU_COLS] = m

    u_ref[...] = proj(0)
    base = POOL_WIDTH
    for c in range(DIFF_WIDTH // MXU_COLS):
        dq_ref[:, c * MXU_COLS:(c + 1) * MXU_COLS] = rope(proj(base + c * MXU_COLS)).astype(BF16)
    base += DIFF_WIDTH
    for ref, row, rotary in ((dk_ref, 0, True), (dv_ref, 1, False)):
        for c in range(DIFF_WIDTH // MXU_COLS):
            z = proj(base + c * MXU_COLS)
            z = (rope(z) if rotary else z).astype(BF16)
            ref[:, c * MXU_COLS:(c + 1) * MXU_COLS] = z
            put_absmax(dstat_ref, row, z, c * MXU_COLS)
        base += DIFF_WIDTH
    mq_ref[...] = rope(proj(base)).astype(BF16)
    base += MOBA_WIDTH
    k = rope(proj(base))
    mk_ref[...] = k.astype(BF16)
    put_absmax(mstat_ref, 0, k.astype(BF16), 0)
    for r in range(tile // MOBA_BLOCK):
        kmean_ref[r] = jnp.mean(k[r * MOBA_BLOCK:(r + 1) * MOBA_BLOCK], axis=0, keepdims=True)
    base += MOBA_WIDTH
    v = proj(base).astype(BF16)
    mv_ref[...] = v
    put_absmax(mstat_ref, 1, v, 0)


def _inproj(x, g, w, cos, sin, layer, seq, cast_weights, ffn=None):
    t, d = x.shape
    tile = FUSED_TILE if ffn is not None else ROW_TILE
    nt = t // tile
    tiles_per_seq = seq // tile
    blocks_per_tile = tile // MOBA_BLOCK
    casts = _SideCasts(cast_weights, layer, nt, lambda i: i)

    def rows(width):
        return pl.BlockSpec((tile, width), lambda i: (i, 0))

    table = pl.BlockSpec((tile, LANES), lambda i: (i % tiles_per_seq, 0))
    widths = (POOL_WIDTH, DIFF_WIDTH, DIFF_WIDTH, DIFF_WIDTH, MOBA_WIDTH, MOBA_WIDTH, MOBA_WIDTH)
    dtypes = (F32, BF16, BF16, BF16, BF16, BF16, BF16)
    out_shape = [jax.ShapeDtypeStruct((t, wd), dt) for wd, dt in zip(widths, dtypes)]
    out_shape.append(jax.ShapeDtypeStruct((t // MOBA_BLOCK, 1, MOBA_WIDTH), F32))
    out_specs = [rows(wd) for wd in widths]
    out_specs.append(pl.BlockSpec((blocks_per_tile, 1, MOBA_WIDTH), lambda i: (i, 0, 0)))
    for wd in (DIFF_WIDTH, MOBA_WIDTH):
        out_shape.append(jax.ShapeDtypeStruct((t // ATT_BLOCK, 2, wd), F32))
        out_specs.append(pl.BlockSpec((blocks_per_tile, 2, wd), lambda i: (i, 0, 0)))
    assert len(out_shape) == _INPROJ_OUTPUTS
    args = [x, g, w, cos, sin]
    in_specs = [rows(d), _layer(g, layer), _layer(w, layer), table, table]
    scratch = []
    if ffn is not None:
        g1, w_in, w_out, g_layer, w_layer = ffn
        args += [g1, w_in, w_out]
        in_specs += [_layer(g1, g_layer), _layer(w_in, w_layer), _layer(w_out, w_layer)]
        out_shape.append(jax.ShapeDtypeStruct((t, d), F32))
        out_specs.append(rows(d))
        scratch.append(pltpu.VMEM((tile, w_out.shape[1]), BF16))
    outs = pl.pallas_call(
        functools.partial(_inproj_body, fused=ffn is not None),
        grid=(nt,),
        in_specs=in_specs + casts.in_specs,
        out_specs=out_specs + casts.out_specs,
        out_shape=out_shape + casts.out_shape,
        scratch_shapes=scratch,
        compiler_params=pltpu.CompilerParams(dimension_semantics=("parallel",), vmem_limit_bytes=VMEM_LIMIT),
        name="ffn_inproj" if ffn is not None else "inproj",
    )(*args, *cast_weights)
    return list(outs)


def _pool_body(u_ref, w_ref, scale_ref, o_ref):
    s = u_ref.shape[0]
    row = lax.broadcasted_iota(jnp.int32, (s, 1), 0)
    groups_per_tile = LANES // POOL_GROUP_DIM
    lane_group = lax.broadcasted_iota(jnp.int32, (1, LANES), 1) // POOL_GROUP_DIM
    tiles = []
    for t in range(POOL_WIDTH // LANES):
        u = u_ref[:, t * LANES:(t + 1) * LANES]
        window_sum, shift, pooled = u, 1, None
        for gi in range(groups_per_tile):
            wdw = POOL_WINDOWS[t * groups_per_tile + gi]
            while shift < wdw:
                shifted = jnp.where(row >= shift, pltpu.roll(window_sum, shift, axis=0), 0.0)
                window_sum = window_sum + shifted
                shift *= 2
            mean = window_sum * (1.0 / jnp.minimum(row + 1, wdw).astype(F32))
            pooled = mean if pooled is None else jnp.where(lane_group == gi, mean, pooled)
        tiles.append(pooled - u)
    mixed = _dot(jnp.concatenate(tiles, axis=1), w_ref[...])
    o_ref[...] = (mixed * scale_ref[...]).astype(BF16)


def _pool(u, w_blockdiag, scale, layer, seq):
    t = u.shape[0]
    blk = pl.BlockSpec((seq, POOL_WIDTH), lambda b: (b, 0))
    return pl.pallas_call(
        _pool_body,
        grid=(t // seq,),
        in_specs=[blk, _layer(w_blockdiag, layer), _layer(scale, layer)],
        out_specs=blk,
        out_shape=jax.ShapeDtypeStruct((t, POOL_WIDTH), BF16),
        compiler_params=pltpu.CompilerParams(dimension_semantics=("parallel",), vmem_limit_bytes=VMEM_LIMIT),
        name="pool",
    )(u, w_blockdiag, scale)


def _channel_is_first(n):
    ch = lax.broadcasted_iota(jnp.int32, (n, 1), 0)
    return (ch // QUARTER) % 2 == 0


def _blk(j):
    return slice(j * ATT_BLOCK, (j + 1) * ATT_BLOCK)


def _transpose_values(v_ref, vt_scr):
    for j in range(v_ref.shape[0] // ATT_BLOCK):
        vt_scr[:, _blk(j)] = v_ref[_blk(j), :].astype(F32).T.astype(BF16)


def _causal_mask():
    key = lax.broadcasted_iota(jnp.int32, (ATT_BLOCK, ATT_BLOCK), 0)
    qry = lax.broadcasted_iota(jnp.int32, (ATT_BLOCK, ATT_BLOCK), 1)
    return key <= qry


def _score_bounds(q_ref, stat_ref):
    nq = stat_ref.shape[1] // 2
    rows, run = [], None
    for j in range(nq):
        m = stat_ref[0, 2 * j:2 * j + 1, :]
        run = m if run is None else jnp.maximum(run, m)
        rows.append(run)
    kabs = jnp.concatenate(rows, axis=0)
    lane_first = (lax.broadcasted_iota(jnp.int32, (1, LANES), 1) // QUARTER) % 2 == 0
    ka = jnp.concatenate([jnp.where(lane_first, kabs, 0.0), jnp.where(lane_first, 0.0, kabs)], axis=0)
    b = lax.dot_general(ka.astype(BF16), jnp.abs(q_ref[...]), (((1,), (1,)), ((), ())), preferred_element_type=F32)
    return b * (1.0 + 2.0 ** -10) + 2.0 ** -10


class _Stream:
    def __init__(self, i, qt, allow, bound, vt, slot, tag):
        self.i, self.qt, self.allow, self.bound, self.vt, self.slot, self.tag = i, qt, allow, bound, vt, slot, tag
        self.mx = self.l = self.acc = None
        self.prestored = False


def _masked_scores(st, j, k_ref, s_scr, causal):
    if st.prestored:
        return s_scr[st.slot, _blk(j), :]
    s = _dot(k_ref[_blk(j), :], st.qt)
    if j == st.i:
        s = jnp.where(causal, s, NEG_INF)
    elif st.allow is not None:
        s = jnp.where(st.allow[j:j + 1, :] > 0.5, s, NEG_INF)
    return s


def _prestore_scores(st, k_ref, s_scr):
    causal = _causal_mask()
    for j in range(st.i + 1):
        s_scr[st.slot, _blk(j), :] = _masked_scores(st, j, k_ref, s_scr, causal)
    st.prestored = True


def _score_phase(st, k_ref, s_scr, causal):
    mx = None
    for j in range(st.i + 1):
        s = _masked_scores(st, j, k_ref, s_scr, causal)
        if not st.prestored:
            s_scr[st.slot, _blk(j), :] = s
        bm = jnp.max(s, axis=0, keepdims=True)
        mx = bm if mx is None else jnp.maximum(mx, bm)
        yield
    st.mx = mx


def _exp_phase(st, s_scr, p_scr):
    for j in range(st.i + 1):
        _emit_probabilities(st, j, jnp.exp2(s_scr[st.slot, _blk(j), :] - st.mx), p_scr)
        yield


def _bounded_phase(st, k_ref, s_scr, p_scr, causal):
    for j in range(st.i + 1):
        _emit_probabilities(st, j, jnp.exp2(_masked_scores(st, j, k_ref, s_scr, causal) - st.bound), p_scr)
        yield


def _emit_probabilities(st, j, p, p_scr):
    p_scr[st.slot, _blk(j), :] = p.astype(BF16)
    bl = jnp.sum(p, axis=0, keepdims=True)
    st.l = bl if j == 0 else st.l + bl


def _value_phase(st, p_scr):
    n = (st.i + 1) * ATT_BLOCK
    st.acc = _dot(st.vt[:, 0:n], p_scr[st.slot, 0:n, :])
    yield


def _run_streams(streams, k_ref, s_scr, p_scr, causal, on_done, bounded):
    def stages(st):
        if bounded:
            return [_bounded_phase(st, k_ref, s_scr, p_scr, causal), _value_phase(st, p_scr)]
        return [_score_phase(st, k_ref, s_scr, causal), _exp_phase(st, s_scr, p_scr), _value_phase(st, p_scr)]

    depth = 2 if bounded else 3
    staged = [stages(st) for st in streams]
    for tick in range(len(streams) + depth - 1):
        live = [staged[tick - d][d] for d in range(depth) if 0 <= tick - d < len(streams)]
        while live:
            for g in list(live):
                if next(g, _DONE) is _DONE:
                    live.remove(g)
        if tick >= depth - 1:
            on_done(streams[tick - depth + 1])


_DONE = object()


def _attention(make_streams, bounds, k_ref, stat_ref, s_scr, p_scr, on_done):
    nq = bounds.shape[0] // 2
    worst = jnp.maximum(bounds[nq - 1:nq, :], bounds[2 * nq - 1:2 * nq, :])
    v_abs = jnp.max(functools.reduce(jnp.maximum, [stat_ref[0, 2 * j + 1:2 * j + 2, :] for j in range(nq)]))
    safe = jnp.logical_and(jnp.max(worst) <= SAFE_BOUND, v_abs <= VALUE_LIMIT)

    @pl.when(safe)
    def _():
        _run_streams(make_streams(), k_ref, s_scr, p_scr, _causal_mask(), on_done, bounded=True)

    @pl.when(jnp.logical_not(safe))
    def _():
        _run_streams(make_streams(), k_ref, s_scr, p_scr, _causal_mask(), on_done, bounded=False)


def _diff_body(q_ref, k_ref, v_ref, stat_ref, lam_ref, g_ref, o_ref, vt_scr, s_scr, p_scr, *, lambda_init):
    lam = lam_ref[...]
    lam_val = (jnp.exp(jnp.sum(lam[0:1] * lam[1:2], keepdims=True))
               - jnp.exp(jnp.sum(lam[2:3] * lam[3:4], keepdims=True)) + lambda_init)
    gain = g_ref[...] * (1.0 - lambda_init)
    nq = q_ref.shape[0] // ATT_BLOCK
    bounds = _score_bounds(q_ref, stat_ref)
    first = _channel_is_first(LANES)

    def build(i, mi, qts):
        if i not in qts:
            qts[i] = q_ref[_blk(i), :].astype(F32).T
        qm = jnp.where(first, qts[i], 0.0) if mi == 0 else jnp.where(first, 0.0, qts[i])
        bound = bounds[mi * nq + i:mi * nq + i + 1, _blk(i)] - SCORE_SHIFT
        return _Stream(i, qm.astype(BF16), None, bound, vt_scr, 2 * (i % 2) + mi, mi)

    head_qts = {}
    head = build(nq - 1, 0, head_qts)
    _prestore_scores(head, k_ref, s_scr)

    def make_streams():
        _transpose_values(v_ref, vt_scr)
        qts = dict(head_qts)
        order = [(i, mi) for i in reversed(range(nq)) for mi in range(2)]
        return [head] + [build(i, mi, qts) for i, mi in order[1:]]

    outs = {}

    def on_done(st):
        outs[st.tag] = st.acc * (1.0 / st.l)
        if st.tag == 1:
            ot = outs[0] - lam_val * outs[1]
            o_ref[_blk(st.i), :] = _rms(ot.T, gain).astype(BF16)

    _attention(make_streams, bounds, k_ref, stat_ref, s_scr, p_scr, on_done)


def _stat_spec(stat):
    return pl.BlockSpec((1, stat.shape[1], LANES), lambda b, h: (b, 0, h))


def _diff_attention(dq, dk, dv, stat, lam, subln, layer, lambda_init, seq):
    t = dq.shape[0]
    blk = pl.BlockSpec((seq, LANES), lambda b, h: (b, h))
    return pl.pallas_call(
        functools.partial(_diff_body, lambda_init=lambda_init),
        grid=(t // seq, DIFF_HEADS),
        in_specs=[blk, blk, blk, _stat_spec(stat), _layer(lam, layer), _layer(subln, layer)],
        out_specs=blk,
        out_shape=jax.ShapeDtypeStruct((t, DIFF_WIDTH), BF16),
        scratch_shapes=[pltpu.VMEM((LANES, seq), BF16), pltpu.VMEM((4, seq, ATT_BLOCK), F32),
                        pltpu.VMEM((4, seq, ATT_BLOCK), BF16)],
        compiler_params=pltpu.CompilerParams(
            dimension_semantics=("parallel", "parallel"), vmem_limit_bytes=VMEM_LIMIT),
        name="diff_attn",
    )(dq, dk, dv, stat, lam, subln)


_MOBA_INPUTS = 5
_MOBA_SCRATCH = 3


def _moba_allow(gate, own, n_sel):
    nb = gate.shape[0]
    blk = lax.broadcasted_iota(jnp.int32, (nb, 1), 0)
    rank = jnp.zeros(gate.shape, jnp.int32)
    for other in range(own):
        g_other = gate[other:other + 1, :]
        beats = (g_other > gate) | ((g_other == gate) & (other < blk))
        rank = rank + jnp.where(beats, 1, 0)
    return jnp.where((blk < own) & (rank < n_sel), 1.0, 0.0)


def _moba_body(*refs):
    n_cast = (len(refs) - _MOBA_INPUTS - 1 - _MOBA_SCRATCH) // 2
    q_ref, k_ref, v_ref, stat_ref, kmean_ref = refs[:_MOBA_INPUTS]
    cast_in = refs[_MOBA_INPUTS:_MOBA_INPUTS + n_cast]
    o_ref = refs[_MOBA_INPUTS + n_cast]
    cast_out = refs[_MOBA_INPUTS + n_cast + 1:_MOBA_INPUTS + 2 * n_cast + 1]
    vt_scr, s_scr, p_scr = refs[-_MOBA_SCRATCH:]
    _cast_slabs(cast_in, cast_out)

    nq = q_ref.shape[0] // ATT_BLOCK
    n_sel = min(MOBA_TOPK, nq - 1)
    bounds = _score_bounds(q_ref, stat_ref)
    first = _channel_is_first(LANES)
    kmean = kmean_ref[0]

    def build(i, hd, qts):
        if i not in qts:
            qts[i] = q_ref[_blk(i), :].astype(F32).T
        qh = jnp.where(first, qts[i], 0.0) if hd == 0 else jnp.where(first, 0.0, qts[i])
        allow = None
        if i > n_sel:
            gate = jnp.dot(kmean, qh, preferred_element_type=F32, precision=lax.Precision.HIGHEST)
            allow = _moba_allow(gate, i, n_sel)
        bound = bounds[hd * nq + i:hd * nq + i + 1, _blk(i)] - SCORE_SHIFT
        vt = vt_scr.at[hd * HEAD_DIM:(hd + 1) * HEAD_DIM, :]
        return _Stream(i, qh.astype(BF16), allow, bound, vt, 2 * (i % 2) + hd, hd)

    head_qts = {}
    head = build(nq - 1, 0, head_qts)
    _prestore_scores(head, k_ref, s_scr)

    def make_streams():
        _transpose_values(v_ref, vt_scr)
        qts = dict(head_qts)
        order = [(i, hd) for i in reversed(range(nq)) for hd in range(2)]
        return [head] + [build(i, hd, qts) for i, hd in order[1:]]

    outs = {}

    def on_done(st):
        outs[st.tag] = st.acc * (1.0 / st.l)
        if st.tag == 1:
            o_ref[_blk(st.i), :] = jnp.concatenate([outs[0], outs[1]], axis=0).T.astype(BF16)

    _attention(make_streams, bounds, k_ref, stat_ref, s_scr, p_scr, on_done)


def _moba_attention(mq, mk, mv, stat, kmean, seq, cast_weights, cast_layer):
    t = mq.shape[0]
    nq = seq // ATT_BLOCK
    pairs = MOBA_WIDTH // LANES
    casts = _SideCasts(cast_weights, cast_layer, (t // seq) * pairs, lambda b, p: b * pairs + p)
    blk = pl.BlockSpec((seq, LANES), lambda b, p: (b, p))
    km = pl.BlockSpec((1, nq, LANES), lambda b, p: (b, 0, p))
    outs = pl.pallas_call(
        _moba_body,
        grid=(t // seq, pairs),
        in_specs=[blk, blk, blk, _stat_spec(stat), km] + casts.in_specs,
        out_specs=[blk] + casts.out_specs,
        out_shape=[jax.ShapeDtypeStruct((t, MOBA_WIDTH), BF16)] + casts.out_shape,
        scratch_shapes=[pltpu.VMEM((LANES, seq), BF16), pltpu.VMEM((4, seq, ATT_BLOCK), F32),
                        pltpu.VMEM((4, seq, ATT_BLOCK), BF16)],
        compiler_params=pltpu.CompilerParams(
            dimension_semantics=("parallel", "parallel"), vmem_limit_bytes=VMEM_LIMIT),
        name="moba_attn",
    )(mq, mk, mv, stat, kmean, *cast_weights)
    return list(outs)


def _inproj_weights(w):
    depth, d, _ = w.shape
    q_scale = HEAD_DIM ** -0.5 * math.log2(math.e)

    def pair_interleave(sec):
        width = sec.shape[-1]
        return sec.reshape(depth, d, width // LANES, 2, 2, QUARTER).swapaxes(3, 4).reshape(depth, d, width)

    parts, base = [], 0
    for width, rotary, scale in ((POOL_WIDTH, False, None),
                                 (DIFF_WIDTH, True, q_scale), (DIFF_WIDTH, True, None), (DIFF_WIDTH, False, None),
                                 (MOBA_WIDTH, True, q_scale), (MOBA_WIDTH, True, None), (MOBA_WIDTH, False, None)):
        sec = w[:, :, base:base + width]
        if rotary:
            sec = pair_interleave(sec)
        if scale is not None:
            sec = sec * scale
        parts.append(sec)
        base += width
    return jnp.concatenate(parts, axis=-1).astype(BF16)


def _rope_tables(seq):
    inv = ROPE_THETA ** (-jnp.arange(0, HEAD_DIM, 2, dtype=F32) / HEAD_DIM)
    ang = jnp.arange(seq, dtype=F32)[:, None] * inv[None, :]
    cos, sin = jnp.cos(ang), jnp.sin(ang)
    return jnp.tile(cos, (1, 4)), jnp.concatenate([-sin, -sin, sin, sin], axis=1)


def kernel(x, ffn1_norm, ffn1_w_in, ffn1_w_out, mix_norm, mix_w_in, mix_w_out, pool_w, pool_scale,
           diff_lambda, diff_subln, ffn2_norm, ffn2_w_in, ffn2_w_out, final_norm):
    batch, seq, d = x.shape
    depth = ffn1_norm.shape[0]
    t = batch * seq
    assert seq % ROW_TILE == 0 and ROW_TILE % MOBA_BLOCK == 0 and ffn1_w_out.shape[1] % FF_CHUNK == 0

    cos, sin = _rope_tables(seq)
    gf = final_norm.reshape(1, d)
    g1, gm, g2 = (g.reshape(depth, 1, d) for g in (ffn1_norm, mix_norm, ffn2_norm))
    w_mix_in = _inproj_weights(mix_w_in)
    groups = len(POOL_WINDOWS)
    pool_bd = jnp.einsum("lgcd,gh->lgchd", pool_w, jnp.eye(groups, dtype=F32)).reshape(depth, POOL_WIDTH, POOL_WIDTH)
    pool_sc = pool_scale.reshape(depth, 1, POOL_WIDTH)
    subln = diff_subln.reshape(depth, 1, LANES)
    nblk = seq // ATT_BLOCK

    w1_in, w1_out = _cast_layer([ffn1_w_in, ffn1_w_out], 0)
    xt = x.reshape(t, d)
    for l in range(depth):
        lambda_init = 0.8 - 0.6 * math.exp(-0.3 * l)
        (u, dq, dk, dv, mq, mk, mv, kmean, dstat, mstat, xt, w2_in, w2_out, w_mix_out) = _inproj(
            xt, gm, w_mix_in, cos, sin, l, seq, [ffn2_w_in, ffn2_w_out, mix_w_out], ffn=(g1, w1_in, w1_out, l, 0))
        dstat = dstat.reshape(batch, 2 * nblk, DIFF_WIDTH)
        mstat = mstat.reshape(batch, 2 * nblk, MOBA_WIDTH)
        ya = _pool(u, pool_bd, pool_sc, l, seq)
        yb = _diff_attention(dq, dk, dv, dstat, diff_lambda, subln, l, lambda_init, seq)
        next_ffn1 = [ffn1_w_in, ffn1_w_out] if l + 1 < depth else []
        yc, *next_w1 = _moba_attention(mq, mk, mv, mstat, kmean.reshape(batch, nblk, MOBA_WIDTH), seq,
                                       next_ffn1, l + 1)
        xt = _ffn(xt, g2, w2_in, w2_out, gf, l, 0, l == depth - 1, mix=(ya, yb, yc, w_mix_out))
        if next_w1:
            w1_in, w1_out = next_w1
    return xt.reshape(batch, seq, d)
```

```python
import functools
import math

import jax
import jax.numpy as jnp
from jax import lax
from jax.experimental import pallas as pl
from jax.experimental.pallas import tpu as pltpu

HEAD_DIM = 64
POOL_WINDOWS = (2, 4, 8, 16)
POOL_GROUP_DIM = 64
POOL_WIDTH = 256
DIFF_WIDTH = 512
DIFF_HEADS = 4
MOBA_WIDTH = 256
MOBA_BLOCK = 256
MOBA_TOPK = 3
ROPE_THETA = 10000.0
NORM_EPS = 1e-6
NEG_INF = -1e30

LANES = 128
BF16_ROWS = 16
MXU_COLS = 256
assert POOL_WIDTH == MXU_COLS and MOBA_WIDTH == MXU_COLS and DIFF_WIDTH % MXU_COLS == 0
QUARTER = HEAD_DIM // 2
ATT_BLOCK = 256
ROW_TILE = 1024
FF_CHUNK = 256
CAST_STEPS = 16
SCORE_SHIFT = 40.0
SAFE_BOUND = 70.0
VALUE_LIMIT = 2.0 ** 60
VMEM_LIMIT = 56 * 1024 * 1024

F32 = jnp.float32
BF16 = jnp.bfloat16


def _dot(a, b):
    return jnp.dot(a, b, preferred_element_type=F32)


def _rms(x, g):
    return x * lax.rsqrt(jnp.mean(x * x, axis=-1, keepdims=True) + NORM_EPS) * g


def _resident(shape):
    nd = len(shape)
    return pl.BlockSpec(shape, lambda *_: (0,) * nd, pipeline_mode=pl.Buffered(1))


def _layer(arr, layer):
    tail = (0,) * (arr.ndim - 1)
    return pl.BlockSpec((None,) + arr.shape[1:], lambda *_: (layer,) + tail, pipeline_mode=pl.Buffered(1))


class _SideCasts:
    def __init__(self, weights, layer, steps, step_of):
        self.in_specs, self.out_shape, self.out_specs = [], [], []
        for w in weights:
            _, a, b = w.shape
            col_parts = next(c for c in range(1, steps + 1) if steps % c == 0 and b % (c * LANES) == 0
                             and a % ((steps // c) * BF16_ROWS) == 0)
            slab = (a // (steps // col_parts), b // col_parts)

            def index(*g, _layer=layer, _cp=col_parts):
                step = step_of(*g)
                return (_layer, step // _cp, step % _cp)

            self.in_specs.append(pl.BlockSpec((None,) + slab, index))
            self.out_shape.append(jax.ShapeDtypeStruct((1, a, b), BF16))
            self.out_specs.append(pl.BlockSpec((None,) + slab, functools.partial(index, _layer=0)))


def _cast_slabs(in_refs, out_refs):
    for src, dst in zip(in_refs, out_refs):
        dst[...] = src[...].astype(BF16)


def _cast_body(*refs):
    _cast_slabs(refs[:len(refs) // 2], refs[len(refs) // 2:])


def _cast_layer(weights, layer):
    casts = _SideCasts(weights, layer, CAST_STEPS, lambda i: i)
    return pl.pallas_call(
        _cast_body,
        grid=(CAST_STEPS,),
        in_specs=casts.in_specs,
        out_specs=casts.out_specs,
        out_shape=casts.out_shape,
        compiler_params=pltpu.CompilerParams(dimension_semantics=("parallel",), vmem_limit_bytes=VMEM_LIMIT),
        name="cast",
    )(*weights)


def _ffn_body(x_ref, g_ref, win_ref, wout_ref, gf_ref, *rest, d_ff, final_norm, with_mix):
    if with_mix:
        ya_ref, yb_ref, yc_ref, wmix_ref, o_ref, u_scr = rest
        lo_b, lo_c = POOL_WIDTH, POOL_WIDTH + DIFF_WIDTH
        x = (x_ref[...] + _dot(ya_ref[...], wmix_ref[0:lo_b, :]) + _dot(yb_ref[...], wmix_ref[lo_b:lo_c, :])
             + _dot(yc_ref[...], wmix_ref[lo_c:, :]))
    else:
        o_ref, u_scr = rest
        x = x_ref[...]
    h = _rms(x, g_ref[...]).astype(BF16)
    for c in range(d_ff // FF_CHUNK):
        lo = c * FF_CHUNK
        a = _dot(h, win_ref[:, lo:lo + FF_CHUNK])
        b = _dot(h, win_ref[:, d_ff + lo:d_ff + lo + FF_CHUNK])
        u_scr[:, lo:lo + FF_CHUNK] = (a / (1.0 + jnp.exp(-a)) * b).astype(BF16)
    y = x + 0.5 * _dot(u_scr[...], wout_ref[...])
    if final_norm:
        y = _rms(y, gf_ref[...])
    o_ref[...] = y


def _ffn(x, g, w_in, w_out, gf, g_layer, w_layer, final_norm, mix=None):
    t, d = x.shape
    d_ff = w_out.shape[1]

    def rows(width):
        return pl.BlockSpec((ROW_TILE, width), lambda i: (i, 0))

    args = [x, g, w_in, w_out, gf]
    in_specs = [rows(d), _layer(g, g_layer), _layer(w_in, w_layer), _layer(w_out, w_layer), _resident((1, d))]
    if mix is not None:
        args.extend(mix)
        in_specs.extend([rows(POOL_WIDTH), rows(DIFF_WIDTH), rows(MOBA_WIDTH), _layer(mix[3], w_layer)])
    return pl.pallas_call(
        functools.partial(_ffn_body, d_ff=d_ff, final_norm=final_norm, with_mix=mix is not None),
        grid=(t // ROW_TILE,),
        in_specs=in_specs,
        out_specs=rows(d),
        out_shape=jax.ShapeDtypeStruct((t, d), F32),
        scratch_shapes=[pltpu.VMEM((ROW_TILE, d_ff), BF16)],
        compiler_params=pltpu.CompilerParams(dimension_semantics=("parallel",), vmem_limit_bytes=VMEM_LIMIT),
        name="mix_ffn" if mix is not None else "ffn",
    )(*args)


_INPROJ_INPUTS = 4
_INPROJ_OUTPUTS = 10


def _inproj_body(*refs):
    n_cast = (len(refs) - _INPROJ_INPUTS - _INPROJ_OUTPUTS) // 2
    x_ref, w_ref, cos_ref, sin_ref = refs[:_INPROJ_INPUTS]
    cast_in = refs[_INPROJ_INPUTS:_INPROJ_INPUTS + n_cast]
    outs = refs[_INPROJ_INPUTS + n_cast:]
    u_ref, dq_ref, dk_ref, dv_ref, mq_ref, mk_ref, mv_ref, kmean_ref, dstat_ref, mstat_ref = outs[:_INPROJ_OUTPUTS]
    _cast_slabs(cast_in, outs[_INPROJ_OUTPUTS:])

    h = _rms(x_ref[...], 1.0).astype(BF16)
    cos = cos_ref[...]
    sin = sin_ref[...]

    def proj(lo):
        return _dot(h, w_ref[:, lo:lo + MXU_COLS])

    def rope(z):
        halves = [z[:, c * LANES:(c + 1) * LANES] for c in range(MXU_COLS // LANES)]
        return jnp.concatenate([zc * cos + pltpu.roll(zc, 2 * QUARTER, axis=1) * sin for zc in halves], axis=1)

    def put_absmax(stat_ref, row, z, lo):
        for r in range(ROW_TILE // ATT_BLOCK):
            m = jnp.max(jnp.abs(z[r * ATT_BLOCK:(r + 1) * ATT_BLOCK].astype(F32)), axis=0, keepdims=True)
            stat_ref[r, row:row + 1, lo:lo + MXU_COLS] = m

    u_ref[...] = proj(0)
    base = POOL_WIDTH
    for c in range(DIFF_WIDTH // MXU_COLS):
        dq_ref[:, c * MXU_COLS:(c + 1) * MXU_COLS] = rope(proj(base + c * MXU_COLS)).astype(BF16)
    base += DIFF_WIDTH
    for ref, row, rotary in ((dk_ref, 0, True), (dv_ref, 1, False)):
        for c in range(DIFF_WIDTH // MXU_COLS):
            z = proj(base + c * MXU_COLS)
            z = (rope(z) if rotary else z).astype(BF16)
            ref[:, c * MXU_COLS:(c + 1) * MXU_COLS] = z
            put_absmax(dstat_ref, row, z, c * MXU_COLS)
        base += DIFF_WIDTH
    mq_ref[...] = rope(proj(base)).astype(BF16)
    base += MOBA_WIDTH
    k = rope(proj(base))
    mk_ref[...] = k.astype(BF16)
    put_absmax(mstat_ref, 0, k.astype(BF16), 0)
    for r in range(ROW_TILE // MOBA_BLOCK):
        kmean_ref[r] = jnp.mean(k[r * MOBA_BLOCK:(r + 1) * MOBA_BLOCK], axis=0, keepdims=True)
    base += MOBA_WIDTH
    v = proj(base).astype(BF16)
    mv_ref[...] = v
    put_absmax(mstat_ref, 1, v, 0)


def _inproj(x, w, cos, sin, layer, seq, cast_weights):
    t, d = x.shape
    nt = t // ROW_TILE
    tiles_per_seq = seq // ROW_TILE
    blocks_per_tile = ROW_TILE // MOBA_BLOCK
    casts = _SideCasts(cast_weights, layer, nt, lambda i: i)

    def rows(width):
        return pl.BlockSpec((ROW_TILE, width), lambda i: (i, 0))

    table = pl.BlockSpec((ROW_TILE, LANES), lambda i: (i % tiles_per_seq, 0))
    widths = (POOL_WIDTH, DIFF_WIDTH, DIFF_WIDTH, DIFF_WIDTH, MOBA_WIDTH, MOBA_WIDTH, MOBA_WIDTH)
    dtypes = (F32, BF16, BF16, BF16, BF16, BF16, BF16)
    out_shape = [jax.ShapeDtypeStruct((t, wd), dt) for wd, dt in zip(widths, dtypes)]
    out_shape.append(jax.ShapeDtypeStruct((t // MOBA_BLOCK, 1, MOBA_WIDTH), F32))
    out_specs = [rows(wd) for wd in widths]
    out_specs.append(pl.BlockSpec((blocks_per_tile, 1, MOBA_WIDTH), lambda i: (i, 0, 0)))
    for wd in (DIFF_WIDTH, MOBA_WIDTH):
        out_shape.append(jax.ShapeDtypeStruct((t // ATT_BLOCK, 2, wd), F32))
        out_specs.append(pl.BlockSpec((blocks_per_tile, 2, wd), lambda i: (i, 0, 0)))
    assert len(out_shape) == _INPROJ_OUTPUTS
    outs = pl.pallas_call(
        _inproj_body,
        grid=(nt,),
        in_specs=[rows(d), _layer(w, layer), table, table] + casts.in_specs,
        out_specs=out_specs + casts.out_specs,
        out_shape=out_shape + casts.out_shape,
        compiler_params=pltpu.CompilerParams(dimension_semantics=("parallel",), vmem_limit_bytes=VMEM_LIMIT),
        name="inproj",
    )(x, w, cos, sin, *cast_weights)
    return list(outs)


def _pool_body(u_ref, w_ref, scale_ref, o_ref):
    s = u_ref.shape[0]
    row = lax.broadcasted_iota(jnp.int32, (s, 1), 0)
    groups_per_tile = LANES // POOL_GROUP_DIM
    lane_group = lax.broadcasted_iota(jnp.int32, (1, LANES), 1) // POOL_GROUP_DIM
    tiles = []
    for t in range(POOL_WIDTH // LANES):
        u = u_ref[:, t * LANES:(t + 1) * LANES]
        window_sum, shift, pooled = u, 1, None
        for gi in range(groups_per_tile):
            wdw = POOL_WINDOWS[t * groups_per_tile + gi]
            while shift < wdw:
                shifted = jnp.where(row >= shift, pltpu.roll(window_sum, shift, axis=0), 0.0)
                window_sum = window_sum + shifted
                shift *= 2
            mean = window_sum * (1.0 / jnp.minimum(row + 1, wdw).astype(F32))
            pooled = mean if pooled is None else jnp.where(lane_group == gi, mean, pooled)
        tiles.append(pooled - u)
    mixed = _dot(jnp.concatenate(tiles, axis=1), w_ref[...])
    o_ref[...] = (mixed * scale_ref[...]).astype(BF16)


def _pool(u, w_blockdiag, scale, layer, seq):
    t = u.shape[0]
    blk = pl.BlockSpec((seq, POOL_WIDTH), lambda b: (b, 0))
    return pl.pallas_call(
        _pool_body,
        grid=(t // seq,),
        in_specs=[blk, _layer(w_blockdiag, layer), _layer(scale, layer)],
        out_specs=blk,
        out_shape=jax.ShapeDtypeStruct((t, POOL_WIDTH), BF16),
        compiler_params=pltpu.CompilerParams(dimension_semantics=("parallel",), vmem_limit_bytes=VMEM_LIMIT),
        name="pool",
    )(u, w_blockdiag, scale)


def _channel_is_first(n):
    ch = lax.broadcasted_iota(jnp.int32, (n, 1), 0)
    return (ch // QUARTER) % 2 == 0


def _blk(j):
    return slice(j * ATT_BLOCK, (j + 1) * ATT_BLOCK)


def _transpose_values(v_ref, vt_scr):
    for j in range(v_ref.shape[0] // ATT_BLOCK):
        vt_scr[:, _blk(j)] = v_ref[_blk(j), :].astype(F32).T.astype(BF16)


def _causal_mask():
    key = lax.broadcasted_iota(jnp.int32, (ATT_BLOCK, ATT_BLOCK), 0)
    qry = lax.broadcasted_iota(jnp.int32, (ATT_BLOCK, ATT_BLOCK), 1)
    return key <= qry


def _score_bounds(q_ref, stat_ref):
    nq = stat_ref.shape[1] // 2
    rows, run = [], None
    for j in range(nq):
        m = stat_ref[0, 2 * j:2 * j + 1, :]
        run = m if run is None else jnp.maximum(run, m)
        rows.append(run)
    kabs = jnp.concatenate(rows, axis=0)
    lane_first = (lax.broadcasted_iota(jnp.int32, (1, LANES), 1) // QUARTER) % 2 == 0
    ka = jnp.concatenate([jnp.where(lane_first, kabs, 0.0), jnp.where(lane_first, 0.0, kabs)], axis=0)
    b = lax.dot_general(ka.astype(BF16), jnp.abs(q_ref[...]), (((1,), (1,)), ((), ())), preferred_element_type=F32)
    return b * (1.0 + 2.0 ** -10) + 2.0 ** -10


class _Stream:
    def __init__(self, i, qt, allow, bound, vt, slot, tag):
        self.i, self.qt, self.allow, self.bound, self.vt, self.slot, self.tag = i, qt, allow, bound, vt, slot, tag
        self.mx = self.l = self.acc = None
        self.prestored = False


def _masked_scores(st, j, k_ref, s_scr, causal):
    if st.prestored:
        return s_scr[st.slot, _blk(j), :]
    s = _dot(k_ref[_blk(j), :], st.qt)
    if j == st.i:
        s = jnp.where(causal, s, NEG_INF)
    elif st.allow is not None:
        s = jnp.where(st.allow[j:j + 1, :] > 0.5, s, NEG_INF)
    return s


def _prestore_scores(st, k_ref, s_scr):
    causal = _causal_mask()
    for j in range(st.i + 1):
        s_scr[st.slot, _blk(j), :] = _masked_scores(st, j, k_ref, s_scr, causal)
    st.prestored = True


def _score_phase(st, k_ref, s_scr, causal):
    mx = None
    for j in range(st.i + 1):
        s = _masked_scores(st, j, k_ref, s_scr, causal)
        if not st.prestored:
            s_scr[st.slot, _blk(j), :] = s
        bm = jnp.max(s, axis=0, keepdims=True)
        mx = bm if mx is None else jnp.maximum(mx, bm)
        yield
    st.mx = mx


def _exp_phase(st, s_scr, p_scr):
    for j in range(st.i + 1):
        _emit_probabilities(st, j, jnp.exp2(s_scr[st.slot, _blk(j), :] - st.mx), p_scr)
        yield


def _bounded_phase(st, k_ref, s_scr, p_scr, causal):
    for j in range(st.i + 1):
        _emit_probabilities(st, j, jnp.exp2(_masked_scores(st, j, k_ref, s_scr, causal) - st.bound), p_scr)
        yield


def _emit_probabilities(st, j, p, p_scr):
    p_scr[st.slot, _blk(j), :] = p.astype(BF16)
    bl = jnp.sum(p, axis=0, keepdims=True)
    st.l = bl if j == 0 else st.l + bl


def _value_phase(st, p_scr):
    n = (st.i + 1) * ATT_BLOCK
    st.acc = _dot(st.vt[:, 0:n], p_scr[st.slot, 0:n, :])
    yield


def _run_streams(streams, k_ref, s_scr, p_scr, causal, on_done, bounded):
    def stages(st):
        if bounded:
            return [_bounded_phase(st, k_ref, s_scr, p_scr, causal), _value_phase(st, p_scr)]
        return [_score_phase(st, k_ref, s_scr, causal), _exp_phase(st, s_scr, p_scr), _value_phase(st, p_scr)]

    depth = 2 if bounded else 3
    staged = [stages(st) for st in streams]
    for tick in range(len(streams) + depth - 1):
        live = [staged[tick - d][d] for d in range(depth) if 0 <= tick - d < len(streams)]
        while live:
            for g in list(live):
                if next(g, _DONE) is _DONE:
                    live.remove(g)
        if tick >= depth - 1:
            on_done(streams[tick - depth + 1])


_DONE = object()


def _attention(make_streams, bounds, k_ref, stat_ref, s_scr, p_scr, on_done):
    nq = bounds.shape[0] // 2
    worst = jnp.maximum(bounds[nq - 1:nq, :], bounds[2 * nq - 1:2 * nq, :])
    v_abs = jnp.max(functools.reduce(jnp.maximum, [stat_ref[0, 2 * j + 1:2 * j + 2, :] for j in range(nq)]))
    safe = jnp.logical_and(jnp.max(worst) <= SAFE_BOUND, v_abs <= VALUE_LIMIT)

    @pl.when(safe)
    def _():
        _run_streams(make_streams(), k_ref, s_scr, p_scr, _causal_mask(), on_done, bounded=True)

    @pl.when(jnp.logical_not(safe))
    def _():
        _run_streams(make_streams(), k_ref, s_scr, p_scr, _causal_mask(), on_done, bounded=False)


def _diff_body(q_ref, k_ref, v_ref, stat_ref, lam_ref, g_ref, o_ref, vt_scr, s_scr, p_scr, *, lambda_init):
    lam = lam_ref[...]
    lam_val = (jnp.exp(jnp.sum(lam[0:1] * lam[1:2], keepdims=True))
               - jnp.exp(jnp.sum(lam[2:3] * lam[3:4], keepdims=True)) + lambda_init)
    gain = g_ref[...] * (1.0 - lambda_init)
    nq = q_ref.shape[0] // ATT_BLOCK
    bounds = _score_bounds(q_ref, stat_ref)
    first = _channel_is_first(LANES)

    def build(i, mi, qts):
        if i not in qts:
            qts[i] = q_ref[_blk(i), :].astype(F32).T
        qm = jnp.where(first, qts[i], 0.0) if mi == 0 else jnp.where(first, 0.0, qts[i])
        bound = bounds[mi * nq + i:mi * nq + i + 1, _blk(i)] - SCORE_SHIFT
        return _Stream(i, qm.astype(BF16), None, bound, vt_scr, 2 * (i % 2) + mi, mi)

    head_qts = {}
    head = build(nq - 1, 0, head_qts)
    _prestore_scores(head, k_ref, s_scr)

    def make_streams():
        _transpose_values(v_ref, vt_scr)
        qts = dict(head_qts)
        order = [(i, mi) for i in reversed(range(nq)) for mi in range(2)]
        return [head] + [build(i, mi, qts) for i, mi in order[1:]]

    outs = {}

    def on_done(st):
        outs[st.tag] = st.acc * (1.0 / st.l)
        if st.tag == 1:
            ot = outs[0] - lam_val * outs[1]
            o_ref[_blk(st.i), :] = _rms(ot.T, gain).astype(BF16)

    _attention(make_streams, bounds, k_ref, stat_ref, s_scr, p_scr, on_done)


def _stat_spec(stat):
    return pl.BlockSpec((1, stat.shape[1], LANES), lambda b, h: (b, 0, h))


def _diff_attention(dq, dk, dv, stat, lam, subln, layer, lambda_init, seq):
    t = dq.shape[0]
    blk = pl.BlockSpec((seq, LANES), lambda b, h: (b, h))
    return pl.pallas_call(
        functools.partial(_diff_body, lambda_init=lambda_init),
        grid=(t // seq, DIFF_HEADS),
        in_specs=[blk, blk, blk, _stat_spec(stat), _layer(lam, layer), _layer(subln, layer)],
        out_specs=blk,
        out_shape=jax.ShapeDtypeStruct((t, DIFF_WIDTH), BF16),
        scratch_shapes=[pltpu.VMEM((LANES, seq), BF16), pltpu.VMEM((4, seq, ATT_BLOCK), F32),
                        pltpu.VMEM((4, seq, ATT_BLOCK), BF16)],
        compiler_params=pltpu.CompilerParams(
            dimension_semantics=("parallel", "parallel"), vmem_limit_bytes=VMEM_LIMIT),
        name="diff_attn",
    )(dq, dk, dv, stat, lam, subln)


_MOBA_INPUTS = 5
_MOBA_SCRATCH = 3


def _moba_allow(gate, own, n_sel):
    nb = gate.shape[0]
    blk = lax.broadcasted_iota(jnp.int32, (nb, 1), 0)
    rank = jnp.zeros(gate.shape, jnp.int32)
    for other in range(own):
        g_other = gate[other:other + 1, :]
        beats = (g_other > gate) | ((g_other == gate) & (other < blk))
        rank = rank + jnp.where(beats, 1, 0)
    return jnp.where((blk < own) & (rank < n_sel), 1.0, 0.0)


def _moba_body(*refs):
    n_cast = (len(refs) - _MOBA_INPUTS - 1 - _MOBA_SCRATCH) // 2
    q_ref, k_ref, v_ref, stat_ref, kmean_ref = refs[:_MOBA_INPUTS]
    cast_in = refs[_MOBA_INPUTS:_MOBA_INPUTS + n_cast]
    o_ref = refs[_MOBA_INPUTS + n_cast]
    cast_out = refs[_MOBA_INPUTS + n_cast + 1:_MOBA_INPUTS + 2 * n_cast + 1]
    vt_scr, s_scr, p_scr = refs[-_MOBA_SCRATCH:]
    _cast_slabs(cast_in, cast_out)

    nq = q_ref.shape[0] // ATT_BLOCK
    n_sel = min(MOBA_TOPK, nq - 1)
    bounds = _score_bounds(q_ref, stat_ref)
    first = _channel_is_first(LANES)
    kmean = kmean_ref[0]

    def build(i, hd, qts):
        if i not in qts:
            qts[i] = q_ref[_blk(i), :].astype(F32).T
        qh = jnp.where(first, qts[i], 0.0) if hd == 0 else jnp.where(first, 0.0, qts[i])
        allow = None
        if i > n_sel:
            gate = jnp.dot(kmean, qh, preferred_element_type=F32, precision=lax.Precision.HIGHEST)
            allow = _moba_allow(gate, i, n_sel)
        bound = bounds[hd * nq + i:hd * nq + i + 1, _blk(i)] - SCORE_SHIFT
        vt = vt_scr.at[hd * HEAD_DIM:(hd + 1) * HEAD_DIM, :]
        return _Stream(i, qh.astype(BF16), allow, bound, vt, 2 * (i % 2) + hd, hd)

    head_qts = {}
    head = build(nq - 1, 0, head_qts)
    _prestore_scores(head, k_ref, s_scr)

    def make_streams():
        _transpose_values(v_ref, vt_scr)
        qts = dict(head_qts)
        order = [(i, hd) for i in reversed(range(nq)) for hd in range(2)]
        return [head] + [build(i, hd, qts) for i, hd in order[1:]]

    outs = {}

    def on_done(st):
        outs[st.tag] = st.acc * (1.0 / st.l)
        if st.tag == 1:
            o_ref[_blk(st.i), :] = jnp.concatenate([outs[0], outs[1]], axis=0).T.astype(BF16)

    _attention(make_streams, bounds, k_ref, stat_ref, s_scr, p_scr, on_done)


def _moba_attention(mq, mk, mv, stat, kmean, seq, cast_weights, cast_layer):
    t = mq.shape[0]
    nq = seq // ATT_BLOCK
    pairs = MOBA_WIDTH // LANES
    casts = _SideCasts(cast_weights, cast_layer, (t // seq) * pairs, lambda b, p: b * pairs + p)
    blk = pl.BlockSpec((seq, LANES), lambda b, p: (b, p))
    km = pl.BlockSpec((1, nq, LANES), lambda b, p: (b, 0, p))
    outs = pl.pallas_call(
        _moba_body,
        grid=(t // seq, pairs),
        in_specs=[blk, blk, blk, _stat_spec(stat), km] + casts.in_specs,
        out_specs=[blk] + casts.out_specs,
        out_shape=[jax.ShapeDtypeStruct((t, MOBA_WIDTH), BF16)] + casts.out_shape,
        scratch_shapes=[pltpu.VMEM((LANES, seq), BF16), pltpu.VMEM((4, seq, ATT_BLOCK), F32),
                        pltpu.VMEM((4, seq, ATT_BLOCK), BF16)],
        compiler_params=pltpu.CompilerParams(
            dimension_semantics=("parallel", "parallel"), vmem_limit_bytes=VMEM_LIMIT),
        name="moba_attn",
    )(mq, mk, mv, stat, kmean, *cast_weights)
    return list(outs)


def _inproj_weights(w, gain):
    depth, d, _ = w.shape
    w = w * gain[:, :, None]
    q_scale = HEAD_DIM ** -0.5 * math.log2(math.e)

    def pair_interleave(sec):
        width = sec.shape[-1]
        return sec.reshape(depth, d, width // LANES, 2, 2, QUARTER).swapaxes(3, 4).reshape(depth, d, width)

    parts, base = [], 0
    for width, rotary, scale in ((POOL_WIDTH, False, None),
                                 (DIFF_WIDTH, True, q_scale), (DIFF_WIDTH, True, None), (DIFF_WIDTH, False, None),
                                 (MOBA_WIDTH, True, q_scale), (MOBA_WIDTH, True, None), (MOBA_WIDTH, False, None)):
        sec = w[:, :, base:base + width]
        if rotary:
            sec = pair_interleave(sec)
        if scale is not None:
            sec = sec * scale
        parts.append(sec)
        base += width
    return jnp.concatenate(parts, axis=-1).astype(BF16)


def _rope_tables(seq):
    inv = ROPE_THETA ** (-jnp.arange(0, HEAD_DIM, 2, dtype=F32) / HEAD_DIM)
    ang = jnp.arange(seq, dtype=F32)[:, None] * inv[None, :]
    cos, sin = jnp.cos(ang), jnp.sin(ang)
    return jnp.tile(cos, (1, 4)), jnp.concatenate([-sin, -sin, sin, sin], axis=1)


def kernel(x, ffn1_norm, ffn1_w_in, ffn1_w_out, mix_norm, mix_w_in, mix_w_out, pool_w, pool_scale,
           diff_lambda, diff_subln, ffn2_norm, ffn2_w_in, ffn2_w_out, final_norm):
    batch, seq, d = x.shape
    depth = ffn1_norm.shape[0]
    t = batch * seq
    assert seq % ROW_TILE == 0 and ROW_TILE % MOBA_BLOCK == 0 and ffn1_w_out.shape[1] % FF_CHUNK == 0

    cos, sin = _rope_tables(seq)
    gf = final_norm.reshape(1, d)
    g1, g2 = (g.reshape(depth, 1, d) for g in (ffn1_norm, ffn2_norm))
    w_mix_in = _inproj_weights(mix_w_in, mix_norm)
    groups = len(POOL_WINDOWS)
    pool_bd = jnp.einsum("lgcd,gh->lgchd", pool_w, jnp.eye(groups, dtype=F32)).reshape(depth, POOL_WIDTH, POOL_WIDTH)
    pool_sc = pool_scale.reshape(depth, 1, POOL_WIDTH)
    subln = diff_subln.reshape(depth, 1, LANES)
    nblk = seq // ATT_BLOCK

    w1_in, w1_out = _cast_layer([ffn1_w_in, ffn1_w_out], 0)
    xt = x.reshape(t, d)
    for l in range(depth):
        lambda_init = 0.8 - 0.6 * math.exp(-0.3 * l)
        xt = _ffn(xt, g1, w1_in, w1_out, gf, l, 0, False)
        (u, dq, dk, dv, mq, mk, mv, kmean, dstat, mstat, w2_in, w2_out, w_mix_out) = _inproj(
            xt, w_mix_in, cos, sin, l, seq, [ffn2_w_in, ffn2_w_out, mix_w_out])
        dstat = dstat.reshape(batch, 2 * nblk, DIFF_WIDTH)
        mstat = mstat.reshape(batch, 2 * nblk, MOBA_WIDTH)
        ya = _pool(u, pool_bd, pool_sc, l, seq)
        yb = _diff_attention(dq, dk, dv, dstat, diff_lambda, subln, l, lambda_init, seq)
        next_ffn1 = [ffn1_w_in, ffn1_w_out] if l + 1 < depth else []
        yc, *next_w1 = _moba_attention(mq, mk, mv, mstat, kmean.reshape(batch, nblk, MOBA_WIDTH), seq,
                                       next_ffn1, l + 1)
        xt = _ffn(xt, g2, w2_in, w2_out, gf, l, 0, l == depth - 1, mix=(ya, yb, yc, w_mix_out))
        if next_w1:
            w1_in, w1_out = next_w1
    return xt.reshape(batch, seq, d)
```
